```python
import jax, jax.numpy as jnp
from jax import lax
import numpy as np

D_MODEL = 4096
BATCH = 1
SEQ = 8192
DEPTH = 2

MLSTM_INNER = 2 * D_MODEL
MLSTM_HEADS = 8
MLSTM_HEAD_DIM = MLSTM_INNER // MLSTM_HEADS
MLSTM_QKV_BLOCK = 4
MLSTM_CONV_WIDTH = 4
MLSTM_CHUNK = 64
SG_CHUNK = 128
SG_GROUPS = 8
SG_WIDTH = 2 * D_MODEL
SG_GROUP_DIM = SG_WIDTH // SG_GROUPS
N_MIXERS = 2
DENSE_FF = 14336
N_EXPERTS = 8
TOP_K = 2
EXPERT_FF = 4096
ALPHA = (2 * DEPTH) ** 0.25
BETA = (8 * DEPTH) ** -0.25
LN_EPS = 1e-5

kernel_name = "hybrid_mlstm_spatialgate_moe_deepnorm"


def layer_norm(x, g, b):
    xf = x.astype(jnp.float32)
    mu = xf.mean(-1, keepdims=True)
    var = jnp.square(xf - mu).mean(-1, keepdims=True)
    return ((xf - mu) * lax.rsqrt(var + LN_EPS) * g + b).astype(x.dtype)


def causal_depthwise_conv(x, w, b):
    out = lax.conv_general_dilated(
        x, w[:, None, :], window_strides=(1,), padding=[(MLSTM_CONV_WIDTH - 1, 0)],
        dimension_numbers=("NWC", "WIO", "NWC"), feature_group_count=x.shape[-1])
    return out + b


def block_diag_proj(x, w):
    B, S, C = x.shape
    xr = x.reshape(B, S, C // MLSTM_QKV_BLOCK, MLSTM_QKV_BLOCK)
    return jnp.einsum("bsnc,ncd->bsnd", xr, w).reshape(B, S, C)


def mlstm_chunkwise(q, k, v, ig, lf):
    B, S, H, dh = q.shape
    L = MLSTM_CHUNK
    nc = S // L

    def to_chunks(t):
        t = t.reshape((B, nc, L, H) + t.shape[3:])
        return jnp.moveaxis(jnp.moveaxis(t, 1, 0), 3, 2)

    causal = jnp.tril(jnp.ones((L, L), dtype=bool))

    def step(carry, inp):
        C, n, m = carry
        qc, kc, vc, igc, lfc = inp
        bcum = jnp.cumsum(lfc, axis=-1)
        dlog = bcum[..., :, None] - bcum[..., None, :] + igc[..., None, :]
        dlog = jnp.where(causal, dlog, -jnp.inf)
        inter_log = bcum + m[..., None]
        m_t = jnp.maximum(inter_log, dlog.max(-1))
        dw = jnp.exp(dlog - m_t[..., None])
        inter_w = jnp.exp(inter_log - m_t)
        scores = jnp.einsum("bhtd,bhsd->bhts", qc, kc) * dw
        num = jnp.einsum("bhts,bhsd->bhtd", scores, vc) \
            + inter_w[..., None] * jnp.einsum("bhtd,bhde->bhte", qc, C)
        den = scores.sum(-1) + inter_w * jnp.einsum("bhtd,bhd->bht", qc, n)
        h = num / jnp.maximum(jnp.abs(den), jnp.exp(-m_t))[..., None]
        b_last = bcum[..., -1]
        wlog = b_last[..., None] - bcum + igc
        m_new = jnp.maximum(b_last + m, wlog.max(-1))
        ws = jnp.exp(wlog - m_new[..., None])
        cw = jnp.exp(b_last + m - m_new)
        kw = kc * ws[..., None]
        C_new = cw[..., None, None] * C + jnp.einsum("bhsd,bhse->bhde", kw, vc)
        n_new = cw[..., None] * n + kw.sum(2)
        return (C_new, n_new, m_new), h

    init = (jnp.zeros((B, H, dh, dh), jnp.float32), jnp.zeros((B, H, dh), jnp.float32),
            jnp.zeros((B, H), jnp.float32))
    _, hs = lax.scan(step, init, (to_chunks(q), to_chunks(k), to_chunks(v), to_chunks(ig), to_chunks(lf)))
    hs = jnp.moveaxis(jnp.moveaxis(hs, 0, 1), 2, 3)
    return hs.reshape(B, S, H, dh)


def mlstm_mixer(x, w_in, conv_w, conv_b, w_q, w_k, w_v, w_gates, b_igate, b_fgate, head_norm_g, skip, w_out):
    B, S, _ = x.shape
    xm, z = jnp.split(x @ w_in, 2, axis=-1)
    xc = jax.nn.silu(causal_depthwise_conv(xm, conv_w, conv_b))
    q = block_diag_proj(xc, w_q)
    k = block_diag_proj(xc, w_k)
    v = block_diag_proj(xm, w_v)
    gates = (q @ w_gates[0] + k @ w_gates[1] + v @ w_gates[2]).astype(jnp.float32)
    ig = gates[..., :MLSTM_HEADS] + b_igate
    lf = jax.nn.log_sigmoid(gates[..., MLSTM_HEADS:] + b_fgate)
    hd = (B, S, MLSTM_HEADS, MLSTM_HEAD_DIM)
    h = mlstm_chunkwise(q.reshape(hd).astype(jnp.float32),
                        k.reshape(hd).astype(jnp.float32) * MLSTM_HEAD_DIM ** -0.5,
                        v.reshape(hd).astype(jnp.float32), ig, lf)
    mu = h.mean(-1, keepdims=True)
    var = jnp.square(h - mu).mean(-1, keepdims=True)
    h = ((h - mu) * lax.rsqrt(var + LN_EPS)).reshape(B, S, MLSTM_INNER) * head_norm_g
    h = jax.nn.sigmoid(z) * (h.astype(x.dtype) + skip * xc)
    return h @ w_out


def spatial_gating_mixer(x, w_in, b_in, norm_g, norm_b, w_s, b_s, w_out):
    B, S, _ = x.shape
    u, v = jnp.split(jax.nn.gelu(x @ w_in + b_in), 2, axis=-1)
    v = layer_norm(v, norm_g, norm_b)
    nc = S // SG_CHUNK
    v = v.reshape(B, nc, SG_CHUNK, SG_GROUPS, SG_GROUP_DIM)
    causal = jnp.tril(jnp.ones((SG_CHUNK, SG_CHUNK), dtype=bool))
    w_causal = jnp.where(causal[None], w_s, 0.0)
    sv = jnp.einsum("gts,bcsgd->bctgd", w_causal, v) + b_s.T[:, :, None]
    return (u * sv.reshape(B, S, SG_WIDTH)) @ w_out


def swiglu(x, w1, w3, w2):
    return (jax.nn.silu(x @ w1) * (x @ w3)) @ w2


def moe_swiglu(x, router_w, router_b, w1, w3, w2):
    B, S, D = x.shape
    xt = x.reshape(B * S, D)
    logits = (xt @ router_w).astype(jnp.float32) + router_b
    top_logits, top_idx = lax.top_k(logits, TOP_K)
    top_w = jax.nn.softmax(top_logits, axis=-1)
    gates = jnp.sum(jax.nn.one_hot(top_idx, N_EXPERTS, dtype=jnp.float32) * top_w[..., None], axis=1)
    y = jnp.zeros_like(xt)
    for e in range(N_EXPERTS):
        y = y + gates[:, e:e + 1].astype(x.dtype) * swiglu(xt, w1[e], w3[e], w2[e])
    return y.reshape(B, S, D)


def setup_inputs(seed: int = 0) -> dict:
    key = jax.random.key(seed)
    ks = iter(jax.random.split(key, 48))

    def nrm(shape, scale):
        return jax.random.normal(next(ks), shape, jnp.float32) * scale

    def gain(n):
        return 1.0 + nrm((n,), 0.02)

    nb = MLSTM_INNER // MLSTM_QKV_BLOCK
    return {
        "x": nrm((BATCH, SEQ, D_MODEL), 1.0),
        "l0_mix_w_in": nrm((D_MODEL, 2 * MLSTM_INNER), D_MODEL ** -0.5),
        "l0_conv_w": nrm((MLSTM_CONV_WIDTH, MLSTM_INNER), MLSTM_CONV_WIDTH ** -0.5),
        "l0_conv_b": nrm((MLSTM_INNER,), 0.01),
        "l0_w_q": nrm((nb, MLSTM_QKV_BLOCK, MLSTM_QKV_BLOCK), MLSTM_QKV_BLOCK ** -0.5),
        "l0_w_k": nrm((nb, MLSTM_QKV_BLOCK, MLSTM_QKV_BLOCK), MLSTM_QKV_BLOCK ** -0.5),
        "l0_w_v": nrm((nb, MLSTM_QKV_BLOCK, MLSTM_QKV_BLOCK), MLSTM_QKV_BLOCK ** -0.5),
        "l0_w_gates": nrm((3, MLSTM_INNER, 2 * MLSTM_HEADS), (3 * MLSTM_INNER) ** -0.5),
        "l0_b_igate": nrm((MLSTM_HEADS,), 0.1),
        "l0_b_fgate": jnp.linspace(3.0, 6.0, MLSTM_HEADS, dtype=jnp.float32) + nrm((MLSTM_HEADS,), 0.1),
        "l0_head_norm_g": gain(MLSTM_INNER),
        "l0_skip": gain(MLSTM_INNER),
        "l0_mix_w_out": nrm((MLSTM_INNER, D_MODEL), MLSTM_INNER ** -0.5 * BETA),
        "l0_ln1_g": gain(D_MODEL),
        "l0_ln1_b": nrm((D_MODEL,), 0.02),
        "l0_ffn_w1": nrm((D_MODEL, DENSE_FF), D_MODEL ** -0.5),
        "l0_ffn_w3": nrm((D_MODEL, DENSE_FF), D_MODEL ** -0.5),
        "l0_ffn_w2": nrm((DENSE_FF, D_MODEL), DENSE_FF ** -0.5 * BETA),
        "l0_ln2_g": gain(D_MODEL),
        "l0_ln2_b": nrm((D_MODEL,), 0.02),
        "l1_mix_w_in": nrm((D_MODEL, 2 * SG_WIDTH), D_MODEL ** -0.5),
        "l1_mix_b_in": nrm((2 * SG_WIDTH,), 0.02),
        "l1_sg_norm_g": gain(SG_WIDTH),
        "l1_sg_norm_b": nrm((SG_WIDTH,), 0.02),
        "l1_sg_w": nrm((SG_GROUPS, SG_CHUNK, SG_CHUNK), SG_CHUNK ** -0.5),
        "l1_sg_b": 1.0 + nrm((SG_GROUPS, SG_CHUNK), 0.02),
        "l1_mix_w_out": nrm((SG_WIDTH, D_MODEL), SG_WIDTH ** -0.5 * BETA),
        "l1_ln1_g": gain(D_MODEL),
        "l1_ln1_b": nrm((D_MODEL,), 0.02),
        "l1_router_w": nrm((D_MODEL, N_EXPERTS), D_MODEL ** -0.5),
        "l1_router_b": nrm((N_EXPERTS,), 0.01),
        "l1_exp_w1": nrm((N_EXPERTS, D_MODEL, EXPERT_FF), D_MODEL ** -0.5),
        "l1_exp_w3": nrm((N_EXPERTS, D_MODEL, EXPERT_FF), D_MODEL ** -0.5),
        "l1_exp_w2": nrm((N_EXPERTS, EXPERT_FF, D_MODEL), EXPERT_FF ** -0.5 * BETA),
        "l1_ln2_g": gain(D_MODEL),
        "l1_ln2_b": nrm((D_MODEL,), 0.02),
    }


def reference(x,
              l0_mix_w_in, l0_conv_w, l0_conv_b, l0_w_q, l0_w_k, l0_w_v, l0_w_gates, l0_b_igate, l0_b_fgate,
              l0_head_norm_g, l0_skip, l0_mix_w_out, l0_ln1_g, l0_ln1_b,
              l0_ffn_w1, l0_ffn_w3, l0_ffn_w2, l0_ln2_g, l0_ln2_b,
              l1_mix_w_in, l1_mix_b_in, l1_sg_norm_g, l1_sg_norm_b, l1_sg_w, l1_sg_b, l1_mix_w_out,
              l1_ln1_g, l1_ln1_b,
              l1_router_w, l1_router_b, l1_exp_w1, l1_exp_w3, l1_exp_w2, l1_ln2_g, l1_ln2_b):
    token_mixers = (
        lambda h: mlstm_mixer(h, l0_mix_w_in, l0_conv_w, l0_conv_b, l0_w_q, l0_w_k, l0_w_v, l0_w_gates,
                              l0_b_igate, l0_b_fgate, l0_head_norm_g, l0_skip, l0_mix_w_out),
        lambda h: spatial_gating_mixer(h, l1_mix_w_in, l1_mix_b_in, l1_sg_norm_g, l1_sg_norm_b,
                                       l1_sg_w, l1_sg_b, l1_mix_w_out),
    )
    channel_mixers = (
        lambda h: swiglu(h, l0_ffn_w1, l0_ffn_w3, l0_ffn_w2),
        lambda h: moe_swiglu(h, l1_router_w, l1_router_b, l1_exp_w1, l1_exp_w3, l1_exp_w2),
    )
    post_norms = ((l0_ln1_g, l0_ln1_b, l0_ln2_g, l0_ln2_b), (l1_ln1_g, l1_ln1_b, l1_ln2_g, l1_ln2_b))
    for layer in range(DEPTH):
        g1, b1, g2, b2 = post_norms[layer]
        x = layer_norm(ALPHA * x + token_mixers[layer % N_MIXERS](x), g1, b1)
        x = layer_norm(ALPHA * x + channel_mixers[layer](x), g2, b2)
    return x
```

```python
import functools

import jax
import jax.numpy as jnp
from jax import lax
from jax.experimental import pallas as pl
from jax.experimental.pallas import tpu as pltpu

F32 = jnp.float32
BF16 = jnp.bfloat16

MLSTM_HEADS = 8
MLSTM_QKV_BLOCK = 4
MLSTM_CONV_WIDTH = 4
SG_CHUNK = 128
SG_GROUPS = 8
N_EXPERTS = 8
TOP_K = 2
DEPTH = 2
ALPHA = (2 * DEPTH) ** 0.25
LN_EPS = 1e-5

LANES = 128
BF16_SUBLANES = 16
MXU_DIM = 256
VMEM_LIMIT_BYTES = 56 * 1024 * 1024

MLSTM_KERNEL_CHUNK = 256
LN_ROWS = 32

MM_ROW_TILE = 1024
MM_COL_TILE = 512
LN_ROW_TILE = 512
LN_K_TILE = 512
CONV_ROW_TILE = 512
CONV_CHANNEL_TILE = 1024
ROUTER_ROW_TILE = 512
EXPERT_ROW_TILE = 512
EXPERT_UP_COL_TILE = 512
EXPERT_DOWN_COL_TILE = 1024
COMBINE_ROW_TILE = 256


def _params(n_axes):
    return pltpu.CompilerParams(dimension_semantics=("arbitrary",) * n_axes,
                                vmem_limit_bytes=VMEM_LIMIT_BYTES)


def _layer_norm_rows(y, g, b):
    mu = jnp.mean(y, axis=-1, keepdims=True)
    d = y - mu
    var = jnp.mean(d * d, axis=-1, keepdims=True)
    return d * lax.rsqrt(var + LN_EPS) * g + b


def _mm_kernel(x_ref, w_ref, o_ref):
    o_ref[...] = jnp.dot(x_ref[...], w_ref[...], preferred_element_type=F32).astype(o_ref.dtype)


def _mm_bias_gelu_kernel(x_ref, w_ref, b_ref, o_ref):
    y = jnp.dot(x_ref[...], w_ref[...], preferred_element_type=F32) + b_ref[...]
    o_ref[...] = jax.nn.gelu(y).astype(o_ref.dtype)


def matmul(x, w, bias=None, *, tm, tn, name):
    M, K = x.shape
    N = w.shape[1]
    in_specs = [pl.BlockSpec((tm, K), lambda i, j: (i, 0)),
                pl.BlockSpec((K, tn), lambda i, j: (0, j))]
    args = [x, w]
    body = _mm_kernel
    if bias is not None:
        in_specs.append(pl.BlockSpec((1, tn), lambda i, j: (0, j)))
        args.append(bias.reshape(1, N).astype(F32))
        body = _mm_bias_gelu_kernel
    return pl.pallas_call(
        body, grid=(M // tm, N // tn), in_specs=in_specs,
        out_specs=pl.BlockSpec((tm, tn), lambda i, j: (i, j)),
        out_shape=jax.ShapeDtypeStruct((M, N), BF16),
        compiler_params=_params(2), name=name)(*args)


def _mm_swiglu_kernel(x_ref, w1_ref, w3_ref, o_ref):
    x = x_ref[...]
    a = jnp.dot(x, w1_ref[...], preferred_element_type=F32)
    b = jnp.dot(x, w3_ref[...], preferred_element_type=F32)
    o_ref[...] = (jax.nn.silu(a) * b).astype(o_ref.dtype)


def matmul_swiglu(x, w1, w3, *, tm, tn, name):
    M, K = x.shape
    N = w1.shape[1]
    wspec = pl.BlockSpec((K, tn), lambda i, j: (0, j))
    return pl.pallas_call(
        _mm_swiglu_kernel, grid=(M // tm, N // tn),
        in_specs=[pl.BlockSpec((tm, K), lambda i, j: (i, 0)), wspec, wspec],
        out_specs=pl.BlockSpec((tm, tn), lambda i, j: (i, j)),
        out_shape=jax.ShapeDtypeStruct((M, N), BF16),
        compiler_params=_params(2), name=name)(x, w1, w3)


def _mm_res_ln_kernel(a_ref, w_ref, r_ref, g_ref, b_ref, o32_ref, *maybe_o16_ref, nk):
    k = pl.program_id(1)

    @pl.when(k == 0)
    def _():
        o32_ref[...] = ALPHA * r_ref[...]

    o32_ref[...] += jnp.dot(a_ref[...], w_ref[...], preferred_element_type=F32)

    @pl.when(k == nk - 1)
    def _():
        g = g_ref[...]
        b = b_ref[...]

        def ln_rows(r, carry):
            rows = pl.ds(pl.multiple_of(r * LN_ROWS, LN_ROWS), LN_ROWS)
            out = _layer_norm_rows(o32_ref[rows, :], g, b)
            o32_ref[rows, :] = out
            for o16_ref in maybe_o16_ref:
                o16_ref[rows, :] = out.astype(BF16)
            return carry

        lax.fori_loop(0, o32_ref.shape[0] // LN_ROWS, ln_rows, 0)


def matmul_residual_ln(a, w, resid, g, b, *, tm, tk, emit_bf16, name):
    M, K = a.shape
    N = w.shape[1]
    nk = K // tk
    row_spec = pl.BlockSpec((tm, N), lambda i, k: (i, 0))
    vec_spec = pl.BlockSpec((1, N), lambda i, k: (0, 0))
    out_shape = [jax.ShapeDtypeStruct((M, N), F32)]
    out_specs = [row_spec]
    if emit_bf16:
        out_shape.append(jax.ShapeDtypeStruct((M, N), BF16))
        out_specs.append(row_spec)
    return pl.pallas_call(
        functools.partial(_mm_res_ln_kernel, nk=nk), grid=(M // tm, nk),
        in_specs=[pl.BlockSpec((tm, tk), lambda i, k: (i, k)),
                  pl.BlockSpec((tk, N), lambda i, k: (k, 0)),
                  row_spec, vec_spec, vec_spec],
        out_specs=out_specs, out_shape=out_shape,
        compiler_params=_params(2), name=name)(a, w, resid, g.reshape(1, N), b.reshape(1, N))


def _shift_rows(cur, halo, s):
    rolled = pltpu.roll(cur, s, 0)
    halo_top = pltpu.roll(halo, s, 0)[:8]
    row = lax.broadcasted_iota(jnp.int32, (8, cur.shape[1]), 0)
    top = jnp.where(row < s, halo_top, rolled[:8])
    return jnp.concatenate([top, rolled[8:]], axis=0)


def _conv_qkv_kernel(cur_ref, halo_ref, cw_ref, cb_ref, wq_ref, wk_ref, wv_ref, wg_ref,
                     xc_ref, q_ref, k_ref, v_ref, gates_ref):
    i = pl.program_id(0)
    c = pl.program_id(1)
    cur_b = cur_ref[...]
    cur = cur_b.astype(F32)
    halo = jnp.where(i > 0, halo_ref[...].astype(F32), 0.0)
    cw = cw_ref[...]
    acc = cw[MLSTM_CONV_WIDTH - 1:MLSTM_CONV_WIDTH, :] * cur + cb_ref[...]
    for s in range(1, MLSTM_CONV_WIDTH):
        j = MLSTM_CONV_WIDTH - 1 - s
        acc = acc + cw[j:j + 1, :] * _shift_rows(cur, halo, s)
    xc_b = jax.nn.silu(acc).astype(BF16)
    xc_ref[...] = xc_b

    def block_diag(x_b, w_ref):
        n = x_b.shape[1] // MXU_DIM
        parts = [jnp.dot(x_b[:, MXU_DIM * j:MXU_DIM * (j + 1)], w_ref[j], preferred_element_type=F32)
                 for j in range(n)]
        return jnp.concatenate(parts, axis=1).astype(BF16)

    q_b = block_diag(xc_b, wq_ref)
    k_b = block_diag(xc_b, wk_ref)
    v_b = block_diag(cur_b, wv_ref)
    q_ref[...] = q_b
    k_ref[...] = k_b
    v_ref[...] = v_b
    part = (jnp.dot(q_b, wg_ref[0], preferred_element_type=F32)
            + jnp.dot(k_b, wg_ref[1], preferred_element_type=F32)
            + jnp.dot(v_b, wg_ref[2], preferred_element_type=F32))

    @pl.when(c == 0)
    def _():
        gates_ref[...] = jnp.zeros_like(gates_ref)

    gates_ref[...] += part


def _expand_block_diag(w):
    nb, blk, _ = w.shape
    per = MXU_DIM // blk
    wt = w.reshape(nb // per, per, blk, blk)
    eye = jnp.eye(per, dtype=w.dtype)
    full = jnp.einsum("tpcd,pq->tpcqd", wt, eye)
    return full.reshape(nb // per, MXU_DIM, MXU_DIM).astype(BF16)


def conv_qkv(xz, conv_w, conv_b, w_q, w_k, w_v, w_gates, *, tm, tc):
    S = xz.shape[0]
    C = conv_w.shape[1]
    ng = w_gates.shape[2]
    wg = jnp.zeros((3, C, LANES), BF16).at[:, :, :ng].set(w_gates.astype(BF16))
    tiles = tc // MXU_DIM
    halo_rows = BF16_SUBLANES
    blk = pl.BlockSpec((tm, tc), lambda i, c: (i, c))
    bd_spec = pl.BlockSpec((tiles, MXU_DIM, MXU_DIM), lambda i, c: (c, 0, 0))
    act = jax.ShapeDtypeStruct((S, C), BF16)
    return pl.pallas_call(
        _conv_qkv_kernel, grid=(S // tm, C // tc),
        in_specs=[blk,
                  pl.BlockSpec((halo_rows, tc), lambda i, c: (jnp.maximum(i * (tm // halo_rows) - 1, 0), c)),
                  pl.BlockSpec((MLSTM_CONV_WIDTH, tc), lambda i, c: (0, c)),
                  pl.BlockSpec((1, tc), lambda i, c: (0, c)),
                  bd_spec, bd_spec, bd_spec,
                  pl.BlockSpec((3, tc, LANES), lambda i, c: (0, c, 0))],
        out_specs=[blk, blk, blk, blk, pl.BlockSpec((tm, LANES), lambda i, c: (i, 0))],
        out_shape=[act, act, act, act, jax.ShapeDtypeStruct((S, LANES), F32)],
        compiler_params=_params(2), name="l0_conv_qkv")(
            xz, xz, conv_w, conv_b.reshape(1, C),
            _expand_block_diag(w_q), _expand_block_diag(w_k), _expand_block_diag(w_v), wg)


def _mlstm_kernel(q_ref, k_ref, v_ref, z_ref, xc_ref, gates_ref, gbias_ref, hng_ref, skip_ref,
                  o_ref, c_ref, cb_ref, n_ref, m_ref, *, heads):
    h = pl.program_id(0)
    c = pl.program_id(1)
    L, dh = q_ref.shape
    scale = dh ** -0.5

    @pl.when(c == 0)
    def _():
        c_ref[...] = jnp.zeros_like(c_ref)
        cb_ref[...] = jnp.zeros_like(cb_ref)
        n_ref[...] = jnp.zeros_like(n_ref)
        m_ref[...] = jnp.zeros_like(m_ref)

    g = gates_ref[...] + gbias_ref[...]
    lane = lax.broadcasted_iota(jnp.int32, g.shape, 1)
    ig = jnp.sum(jnp.where(lane == h, g, 0.0), axis=1, keepdims=True)
    fpre = jnp.sum(jnp.where(lane == heads + h, g, 0.0), axis=1, keepdims=True)
    lf = jnp.minimum(fpre, 0.0) - jnp.log1p(jnp.exp(-jnp.abs(fpre)))

    row = lax.broadcasted_iota(jnp.int32, (L, L), 0)
    col = lax.broadcasted_iota(jnp.int32, (L, L), 1)
    causal = col <= row
    bcum = jnp.dot(causal.astype(F32), jnp.broadcast_to(lf, (L, LANES)),
                   precision=lax.Precision.HIGHEST, preferred_element_type=F32)[:, :1]
    r_row = jnp.transpose(jnp.broadcast_to(ig - bcum, (L, LANES)))[:1, :]

    m_prev = m_ref[:1, :1]
    dlog = jnp.where(causal, bcum + r_row, -jnp.inf)
    inter_log = bcum + m_prev
    m_t = jnp.maximum(inter_log, jnp.max(dlog, axis=1, keepdims=True))
    dw = jnp.exp(dlog - m_t)
    inter_w = jnp.exp(inter_log - m_t)

    qb = q_ref[...]
    kb = k_ref[...]
    vb = v_ref[...]
    scores = lax.dot_general(qb, kb, (((1,), (1,)), ((), ())), preferred_element_type=F32) * (dw * scale)
    num = (jnp.dot(scores.astype(BF16), vb, preferred_element_type=F32)
           + inter_w * jnp.dot(qb, cb_ref[...], preferred_element_type=F32))
    qn = jnp.sum(qb.astype(F32) * n_ref[:1, :], axis=1, keepdims=True)
    den = jnp.sum(scores, axis=1, keepdims=True) + inter_w * qn
    hh = num / jnp.maximum(jnp.abs(den), jnp.exp(-m_t))

    b_last = bcum[L - 1:L, :]
    wlog = b_last - bcum + ig
    m_new = jnp.maximum(b_last + m_prev, jnp.max(wlog, axis=0, keepdims=True))
    ws = jnp.exp(wlog - m_new)
    cw = jnp.exp(b_last + m_prev - m_new)
    kw = kb.astype(F32) * (ws * scale)
    c_new = cw * c_ref[...] + lax.dot_general(kw.astype(BF16), vb, (((0,), (0,)), ((), ())),
                                              preferred_element_type=F32)
    c_ref[...] = c_new
    cb_ref[...] = c_new.astype(BF16)
    n_ref[...] = jnp.broadcast_to(cw * n_ref[:1, :] + jnp.sum(kw, axis=0, keepdims=True), n_ref.shape)
    m_ref[...] = jnp.broadcast_to(m_new, m_ref.shape)

    mu = jnp.mean(hh, axis=1, keepdims=True)
    d = hh - mu
    var = jnp.mean(d * d, axis=1, keepdims=True)
    hn = d * lax.rsqrt(var + LN_EPS) * hng_ref[...]
    out = jax.nn.sigmoid(z_ref[...].astype(F32)) * (hn + skip_ref[...] * xc_ref[...].astype(F32))
    o_ref[...] = out.astype(o_ref.dtype)


def mlstm(q, k, v, xz, xc, gates, b_igate, b_fgate, head_norm_g, skip, *, chunk):
    S, C = q.shape
    heads = b_igate.shape[0]
    dh = C // heads
    gbias = jnp.zeros((1, LANES), F32).at[0, :heads].set(b_igate).at[0, heads:2 * heads].set(b_fgate)
    blk = pl.BlockSpec((chunk, dh), lambda h, c: (c, h))
    vec = pl.BlockSpec((1, dh), lambda h, c: (0, h))
    return pl.pallas_call(
        functools.partial(_mlstm_kernel, heads=heads), grid=(heads, S // chunk),
        in_specs=[blk, blk, blk,
                  pl.BlockSpec((chunk, dh), lambda h, c: (c, heads + h)),
                  blk,
                  pl.BlockSpec((chunk, LANES), lambda h, c: (c, 0)),
                  pl.BlockSpec((1, LANES), lambda h, c: (0, 0)),
                  vec, vec],
        out_specs=blk,
        out_shape=jax.ShapeDtypeStruct((S, C), BF16),
        scratch_shapes=[pltpu.VMEM((dh, dh), F32), pltpu.VMEM((dh, dh), BF16),
                        pltpu.VMEM((8, dh), F32), pltpu.VMEM((8, LANES), F32)],
        compiler_params=_params(2), name="l0_mlstm")(
            q, k, v, xz, xc, gates, gbias, head_norm_g.reshape(1, C), skip.reshape(1, C))


def _spatial_gate_kernel(u_ref, v_ref, ng_ref, nb_ref, ws_ref, bs_ref, o_ref, *, groups):
    T = v_ref.shape[0]
    gd = v_ref.shape[1] // groups
    vn = _layer_norm_rows(v_ref[...].astype(F32), ng_ref[...], nb_ref[...]).astype(BF16)
    row = lax.broadcasted_iota(jnp.int32, (T, T), 0)
    col = lax.broadcasted_iota(jnp.int32, (T, T), 1)
    causal = col <= row
    bs = bs_ref[...]
    for g in range(groups):
        wc = jnp.where(causal, ws_ref[g], 0.0).astype(BF16)
        sv = jnp.dot(wc, vn[:, g * gd:(g + 1) * gd], preferred_element_type=F32) + bs[:, g:g + 1]
        o_ref[:, g * gd:(g + 1) * gd] = (u_ref[:, g * gd:(g + 1) * gd].astype(F32) * sv).astype(o_ref.dtype)


def spatial_gate(uv, norm_g, norm_b, w_s, b_s):
    S = uv.shape[0]
    W = norm_g.shape[0]
    groups, T, _ = w_s.shape
    bs_cols = jnp.zeros((T, LANES), F32).at[:, :groups].set(b_s.T)
    return pl.pallas_call(
        functools.partial(_spatial_gate_kernel, groups=groups), grid=(S // T,),
        in_specs=[pl.BlockSpec((T, W), lambda c: (c, 0)),
                  pl.BlockSpec((T, W), lambda c: (c, 1)),
                  pl.BlockSpec((1, W), lambda c: (0, 0)),
                  pl.BlockSpec((1, W), lambda c: (0, 0)),
                  pl.BlockSpec((groups, T, T), lambda c: (0, 0, 0)),
                  pl.BlockSpec((T, LANES), lambda c: (0, 0))],
        out_specs=pl.BlockSpec((T, W), lambda c: (c, 0)),
        out_shape=jax.ShapeDtypeStruct((S, W), BF16),
        compiler_params=_params(1), name="l1_spatial_gate")(
            uv, uv, norm_g.reshape(1, W), norm_b.reshape(1, W), w_s, bs_cols)


INFO_IDX0, INFO_IDX1, INFO_GATE0, INFO_GATE1, INFO_RANK0, INFO_RANK1 = range(6)


def _router_kernel(x_ref, w_ref, b_ref, info_ref, count_ref, carry_ref, *, n_experts):
    i = pl.program_id(0)
    tm = x_ref.shape[0]

    @pl.when(i == 0)
    def _():
        carry_ref[...] = jnp.zeros_like(carry_ref)

    logits = jnp.dot(x_ref[...], w_ref[...], precision=lax.Precision.HIGHEST,
                     preferred_element_type=F32) + b_ref[...]
    lane = lax.broadcasted_iota(jnp.int32, logits.shape, 1).astype(F32)
    neg_inf = -jnp.inf
    logits = jnp.where(lane < n_experts, logits, neg_inf)
    m0 = jnp.max(logits, axis=1, keepdims=True)
    i0 = jnp.min(jnp.where(logits == m0, lane, float(LANES)), axis=1, keepdims=True)
    rest = jnp.where(lane == i0, neg_inf, logits)
    m1 = jnp.max(rest, axis=1, keepdims=True)
    i1 = jnp.min(jnp.where(rest == m1, lane, float(LANES)), axis=1, keepdims=True)
    e1 = jnp.exp(m1 - m0)
    denom = 1.0 + e1
    g0 = 1.0 / denom
    g1 = e1 / denom
    hot0 = lane == i0
    hot1 = lane == i1
    member = jnp.where(hot0 | hot1, 1.0, 0.0)
    row = lax.broadcasted_iota(jnp.int32, (tm, tm), 0)
    col = lax.broadcasted_iota(jnp.int32, (tm, tm), 1)
    before = jnp.where(col < row, 1.0, 0.0).astype(BF16)
    rank = jnp.dot(before, member.astype(BF16), preferred_element_type=F32) + carry_ref[:1, :]
    r0 = jnp.sum(jnp.where(hot0, rank, 0.0), axis=1, keepdims=True)
    r1 = jnp.sum(jnp.where(hot1, rank, 0.0), axis=1, keepdims=True)
    info = jnp.zeros_like(logits)
    for slot, val in ((INFO_IDX0, i0), (INFO_IDX1, i1), (INFO_GATE0, g0), (INFO_GATE1, g1),
                      (INFO_RANK0, r0), (INFO_RANK1, r1)):
        info = jnp.where(lane == slot, val, info)
    info_ref[...] = info
    total = carry_ref[...] + jnp.sum(member, axis=0, keepdims=True)
    carry_ref[...] = total
    count_ref[...] = total


def router(x, router_w, router_b, *, tm):
    S, D = x.shape
    E = router_w.shape[1]
    w = jnp.zeros((D, LANES), F32).at[:, :E].set(router_w)
    b = jnp.zeros((1, LANES), F32).at[0, :E].set(router_b)
    return pl.pallas_call(
        functools.partial(_router_kernel, n_experts=E), grid=(S // tm,),
        in_specs=[pl.BlockSpec((tm, D), lambda i: (i, 0)),
                  pl.BlockSpec((D, LANES), lambda i: (0, 0)),
                  pl.BlockSpec((1, LANES), lambda i: (0, 0))],
        out_specs=[pl.BlockSpec((tm, LANES), lambda i: (i, 0)),
                   pl.BlockSpec((8, LANES), lambda i: (0, 0))],
        out_shape=[jax.ShapeDtypeStruct((S, LANES), F32), jax.ShapeDtypeStruct((8, LANES), F32)],
        scratch_shapes=[pltpu.VMEM((8, LANES), F32)],
        compiler_params=_params(1), name="l1_router")(x, w, b)


def _row_copy(src_hbm, dst_vmem, sem, src_row, dst_row):
    return pltpu.make_async_copy(src_hbm.at[pl.ds(src_row, 1), :], dst_vmem.at[pl.ds(dst_row, 1), :], sem)


def _gather_rows_kernel(row_src_ref, n_used_ref, x_hbm, o_ref, buf_ref, sem):
    t = pl.program_id(0)
    tm = buf_ref.shape[0]

    @pl.when(t < n_used_ref[0])
    def _():
        def start(r, carry):
            _row_copy(x_hbm, buf_ref, sem, row_src_ref[t * tm + r], r).start()
            return carry

        def wait(r, carry):
            _row_copy(x_hbm, buf_ref, sem, 0, r).wait()
            return carry

        lax.fori_loop(0, tm, start, 0)
        lax.fori_loop(0, tm, wait, 0)
        o_ref[...] = buf_ref[...].astype(o_ref.dtype)

    @pl.when(t >= n_used_ref[0])
    def _():
        o_ref[...] = jnp.zeros_like(o_ref)


def gather_rows(x, row_src, n_used, *, tm):
    P = row_src.shape[0]
    D = x.shape[1]
    grid_spec = pltpu.PrefetchScalarGridSpec(
        num_scalar_prefetch=2, grid=(P // tm,),
        in_specs=[pl.BlockSpec(memory_space=pl.ANY)],
        out_specs=pl.BlockSpec((tm, D), lambda t, rs, nu: (t, 0)),
        scratch_shapes=[pltpu.VMEM((tm, D), F32), pltpu.SemaphoreType.DMA(())])
    return pl.pallas_call(
        _gather_rows_kernel, grid_spec=grid_spec,
        out_shape=jax.ShapeDtypeStruct((P, D), BF16),
        compiler_params=_params(1), name="l1_moe_gather")(row_src, n_used, x)


def _grouped_kernel(te_ref, n_used_ref, *refs, tile_body):
    o_ref = refs[-1]
    used = pl.program_id(1) < n_used_ref[0]

    @pl.when(used)
    def _():
        tile_body(*refs)

    @pl.when(jnp.logical_not(used))
    def _():
        o_ref[...] = jnp.zeros_like(o_ref)


def _grouped_call(tile_body, x, weights, tile_expert, n_used, *, tm, tn, out_dtype, name):
    P, K = x.shape
    N = weights[0].shape[2]

    def row_map(j, t, te, nu):
        return (jnp.minimum(t, nu[0] - 1), 0)

    def out_map(j, t, te, nu):
        return (t, j)

    def w_map(j, t, te, nu):
        return (te[t], 0, j)

    grid_spec = pltpu.PrefetchScalarGridSpec(
        num_scalar_prefetch=2, grid=(N // tn, P // tm),
        in_specs=[pl.BlockSpec((tm, K), row_map)] + [pl.BlockSpec((None, K, tn), w_map)] * len(weights),
        out_specs=pl.BlockSpec((tm, tn), out_map))
    return pl.pallas_call(
        functools.partial(_grouped_kernel, tile_body=tile_body), grid_spec=grid_spec,
        out_shape=jax.ShapeDtypeStruct((P, N), out_dtype),
        compiler_params=_params(2), name=name)(tile_expert, n_used, x, *weights)


def _combine_ln_kernel(pos0_ref, pos1_ref, y_hbm, x_ref, info_ref, g_ref, b_ref, o_ref,
                       buf0_ref, buf1_ref, sem):
    i = pl.program_id(0)
    tm = x_ref.shape[0]

    def start(r, carry):
        _row_copy(y_hbm, buf0_ref, sem, pos0_ref[i * tm + r], r).start()
        _row_copy(y_hbm, buf1_ref, sem, pos1_ref[i * tm + r], r).start()
        return carry

    def wait(r, carry):
        _row_copy(y_hbm, buf0_ref, sem, 0, r).wait()
        _row_copy(y_hbm, buf1_ref, sem, 0, r).wait()
        return carry

    lax.fori_loop(0, tm, start, 0)
    lax.fori_loop(0, tm, wait, 0)
    info = info_ref[...]
    g0 = info[:, INFO_GATE0:INFO_GATE0 + 1]
    g1 = info[:, INFO_GATE1:INFO_GATE1 + 1]
    y = g0 * buf0_ref[...] + g1 * buf1_ref[...]
    o_ref[...] = _layer_norm_rows(ALPHA * x_ref[...] + y, g_ref[...], b_ref[...])


def combine_ln(y_rows, pos0, pos1, info, x, g, b, *, tm):
    S, D = x.shape
    grid_spec = pltpu.PrefetchScalarGridSpec(
        num_scalar_prefetch=2, grid=(S // tm,),
        in_specs=[pl.BlockSpec(memory_space=pl.ANY),
                  pl.BlockSpec((tm, D), lambda i, p0, p1: (i, 0)),
                  pl.BlockSpec((tm, LANES), lambda i, p0, p1: (i, 0)),
                  pl.BlockSpec((1, D), lambda i, p0, p1: (0, 0)),
                  pl.BlockSpec((1, D), lambda i, p0, p1: (0, 0))],
        out_specs=pl.BlockSpec((tm, D), lambda i, p0, p1: (i, 0)),
        scratch_shapes=[pltpu.VMEM((tm, D), F32), pltpu.VMEM((tm, D), F32), pltpu.SemaphoreType.DMA(())])
    return pl.pallas_call(
        _combine_ln_kernel, grid_spec=grid_spec,
        out_shape=jax.ShapeDtypeStruct((S, D), F32),
        compiler_params=_params(1), name="l1_moe_combine_ln")(
            pos0, pos1, y_rows, x, info, g.reshape(1, D), b.reshape(1, D))


def moe_layer(x, router_w, router_b, w1, w3, w2, ln_g, ln_b, *, tm_route, tm, tn_up, tn_down, tm_combine):
    S, D = x.shape
    E = router_w.shape[1]
    info, counts = router(x, router_w, router_b, tm=tm_route)
    idx0 = info[:, INFO_IDX0].astype(jnp.int32)
    idx1 = info[:, INFO_IDX1].astype(jnp.int32)
    count = counts[0, :E].astype(jnp.int32)
    tiles = (count + tm - 1) // tm
    tile_end = jnp.cumsum(tiles)
    offset = (tile_end - tiles) * tm
    n_used = tile_end[-1:]
    pos0 = offset[idx0] + info[:, INFO_RANK0].astype(jnp.int32)
    pos1 = offset[idx1] + info[:, INFO_RANK1].astype(jnp.int32)
    n_rows = TOP_K * S + E * tm
    n_tiles = n_rows // tm
    tile_id = jnp.minimum(jnp.arange(n_tiles, dtype=jnp.int32), n_used[0] - 1)
    tile_expert = jnp.sum(tile_id[:, None] >= tile_end[None, :], axis=1).astype(jnp.int32)
    token = jnp.arange(S, dtype=jnp.int32)
    row_src = jnp.zeros((n_rows,), jnp.int32).at[pos0].set(token).at[pos1].set(token)

    xs = gather_rows(x, row_src, n_used, tm=tm)
    hs = _grouped_call(_mm_swiglu_kernel, xs, (w1, w3), tile_expert, n_used,
                       tm=tm, tn=tn_up, out_dtype=BF16, name="l1_moe_up")
    ys = _grouped_call(_mm_kernel, hs, (w2,), tile_expert, n_used,
                       tm=tm, tn=tn_down, out_dtype=F32, name="l1_moe_down")
    return combine_ln(ys, pos0, pos1, info, x, ln_g, ln_b, tm=tm_combine)


def kernel(x, l0_mix_w_in, l0_conv_w, l0_conv_b, l0_w_q, l0_w_k, l0_w_v, l0_w_gates, l0_b_igate, l0_b_fgate, l0_head_norm_g, l0_skip, l0_mix_w_out, l0_ln1_g, l0_ln1_b, l0_ffn_w1, l0_ffn_w3, l0_ffn_w2, l0_ln2_g, l0_ln2_b, l1_mix_w_in, l1_mix_b_in, l1_sg_norm_g, l1_sg_norm_b, l1_sg_w, l1_sg_b, l1_mix_w_out, l1_ln1_g, l1_ln1_b, l1_router_w, l1_router_b, l1_exp_w1, l1_exp_w3, l1_exp_w2, l1_ln2_g, l1_ln2_b):
    B, S, D = x.shape
    x0 = x.reshape(B * S, D)

    def bf(w):
        return w.astype(BF16)

    tm_mm = min(MM_ROW_TILE, S)
    tm_ln = min(LN_ROW_TILE, S)
    xz = matmul(bf(x0), bf(l0_mix_w_in), tm=tm_mm, tn=MM_COL_TILE, name="l0_in_proj")
    xc, q, k, v, gates = conv_qkv(xz, l0_conv_w, l0_conv_b, l0_w_q, l0_w_k, l0_w_v, l0_w_gates,
                                  tm=min(CONV_ROW_TILE, S), tc=min(CONV_CHANNEL_TILE, l0_conv_w.shape[1]))
    hg = mlstm(q, k, v, xz, xc, gates, l0_b_igate, l0_b_fgate, l0_head_norm_g, l0_skip, chunk=MLSTM_KERNEL_CHUNK)
    x1, x1b = matmul_residual_ln(hg, bf(l0_mix_w_out), x0, l0_ln1_g, l0_ln1_b,
                                 tm=tm_ln, tk=LN_K_TILE, emit_bf16=True, name="l0_out_proj_ln")
    h = matmul_swiglu(x1b, bf(l0_ffn_w1), bf(l0_ffn_w3), tm=tm_mm, tn=MM_COL_TILE, name="l0_ffn_up")
    x2, x2b = matmul_residual_ln(h, bf(l0_ffn_w2), x1, l0_ln2_g, l0_ln2_b,
                                 tm=tm_ln, tk=LN_K_TILE, emit_bf16=True, name="l0_ffn_down_ln")
    uv = matmul(x2b, bf(l1_mix_w_in), l1_mix_b_in, tm=tm_mm, tn=MM_COL_TILE, name="l1_in_proj_gelu")
    gated = spatial_gate(uv, l1_sg_norm_g, l1_sg_norm_b, l1_sg_w, l1_sg_b)
    (x3,) = matmul_residual_ln(gated, bf(l1_mix_w_out), x2, l1_ln1_g, l1_ln1_b,
                               tm=tm_ln, tk=LN_K_TILE, emit_bf16=False, name="l1_out_proj_ln")
    y = moe_layer(x3, l1_router_w, l1_router_b, bf(l1_exp_w1), bf(l1_exp_w3), bf(l1_exp_w2), l1_ln2_g, l1_ln2_b,
                  tm_route=min(ROUTER_ROW_TILE, S), tm=EXPERT_ROW_TILE,
                  tn_up=min(EXPERT_UP_COL_TILE, l1_exp_w1.shape[2]), tn_down=min(EXPERT_DOWN_COL_TILE, D),
                  tm_combine=min(COMBINE_ROW_TILE, S))
    return y.reshape(B, S, D)
```

```python
import functools

import jax
import jax.numpy as jnp
from jax import lax
from jax.experimental import pallas as pl
from jax.experimental.pallas import tpu as pltpu

F32 = jnp.float32
BF16 = jnp.bfloat16

MLSTM_HEADS = 8
MLSTM_QKV_BLOCK = 4
MLSTM_CONV_WIDTH = 4
SG_CHUNK = 128
SG_GROUPS = 8
N_EXPERTS = 8
TOP_K = 2
DEPTH = 2
ALPHA = (2 * DEPTH) ** 0.25
LN_EPS = 1e-5

LANES = 128
BF16_SUBLANES = 16
MXU_DIM = 256
VMEM_LIMIT_BYTES = 56 * 1024 * 1024

MLSTM_KERNEL_CHUNK = 256
MLSTM_HEADS_PER_STEP = 2
LN_ROWS = 8
LN_UNROLL = 8

MM_ROW_TILE = 1024
MM_COL_TILE = 512
SWIGLU_COL_TILE = 256
LN_ROW_TILE = 512
LN_K_TILE = 1024
CONV_ROW_TILE = 512
CONV_CHANNEL_TILE = 1024
ROUTER_ROW_TILE = 512
EXPERT_ROW_TILE = 512
EXPERT_UP_COL_TILE = 512
EXPERT_DOWN_COL_TILE = 1024
COMBINE_ROW_TILE = 256


def _params(n_axes):
    return pltpu.CompilerParams(dimension_semantics=("arbitrary",) * n_axes,
                                vmem_limit_bytes=VMEM_LIMIT_BYTES)


def _layer_norm_rows(y, g, b):
    mu = jnp.mean(y, axis=-1, keepdims=True)
    d = y - mu
    var = jnp.mean(d * d, axis=-1, keepdims=True)
    return d * lax.rsqrt(var + LN_EPS) * g + b


def _pack_bf16_pair(lo, hi):
    lo_bits = lax.bitcast_convert_type(lo.astype(BF16).astype(F32), jnp.uint32) >> 16
    hi_bits = lax.bitcast_convert_type(hi.astype(BF16).astype(F32), jnp.uint32) & jnp.uint32(0xFFFF0000)
    return hi_bits | lo_bits


def _unpack_bf16_pair(p):
    lo = lax.bitcast_convert_type(p << 16, F32)
    hi = lax.bitcast_convert_type(p & jnp.uint32(0xFFFF0000), F32)
    return lo, hi


def _mm_kernel(x_ref, w_ref, o_ref):
    o_ref[...] = jnp.dot(x_ref[...], w_ref[...].astype(BF16), preferred_element_type=F32).astype(o_ref.dtype)


def _mm_bias_gelu_kernel(x_ref, w_ref, b_ref, o_ref):
    y = jnp.dot(x_ref[...], w_ref[...].astype(BF16), preferred_element_type=F32) + b_ref[...]
    o_ref[...] = jax.nn.gelu(y).astype(o_ref.dtype)


def matmul(x, w, bias=None, *, tm, tn, name):
    M, K = x.shape
    N = w.shape[1]
    in_specs = [pl.BlockSpec((tm, K), lambda i, j: (i, 0)),
                pl.BlockSpec((K, tn), lambda i, j: (0, j))]
    args = [x, w]
    body = _mm_kernel
    if bias is not None:
        in_specs.append(pl.BlockSpec((1, tn), lambda i, j: (0, j)))
        args.append(bias.reshape(1, N).astype(F32))
        body = _mm_bias_gelu_kernel
    return pl.pallas_call(
        body, grid=(M // tm, N // tn), in_specs=in_specs,
        out_specs=pl.BlockSpec((tm, tn), lambda i, j: (i, j)),
        out_shape=jax.ShapeDtypeStruct((M, N), BF16),
        compiler_params=_params(2), name=name)(*args)


def _mm_swiglu_kernel(x_ref, w1_ref, w3_ref, o_ref):
    x = x_ref[...]
    a = jnp.dot(x, w1_ref[...].astype(BF16), preferred_element_type=F32)
    b = jnp.dot(x, w3_ref[...].astype(BF16), preferred_element_type=F32)
    o_ref[...] = (jax.nn.silu(a) * b).astype(o_ref.dtype)


def matmul_swiglu(x, w1, w3, *, tm, tn, name):
    M, K = x.shape
    N = w1.shape[1]
    wspec = pl.BlockSpec((K, tn), lambda i, j: (0, j))
    return pl.pallas_call(
        _mm_swiglu_kernel, grid=(M // tm, N // tn),
        in_specs=[pl.BlockSpec((tm, K), lambda i, j: (i, 0)), wspec, wspec],
        out_specs=pl.BlockSpec((tm, tn), lambda i, j: (i, j)),
        out_shape=jax.ShapeDtypeStruct((M, N), BF16),
        compiler_params=_params(2), name=name)(x, w1, w3)


def _mm_res_ln_kernel(a_ref, w_ref, r_ref, g_ref, b_ref, o32_ref, o2_ref, mu_ref, rstd_ref, *, nk):
    k = pl.program_id(1)

    @pl.when(k == 0)
    def _():
        o32_ref[...] = ALPHA * r_ref[...]

    o32_ref[...] += jnp.dot(a_ref[...], w_ref[...], preferred_element_type=F32)

    @pl.when(k == nk - 1)
    def _():
        n_groups = o32_ref.shape[0] // LN_ROWS
        half = o32_ref.shape[1] // 2

        def group(r):
            return pl.ds(pl.multiple_of(r * LN_ROWS, LN_ROWS), LN_ROWS)

        def stats(r, carry):
            y = o32_ref[group(r), :]
            mu = jnp.mean(y, axis=-1, keepdims=True)
            d = y - mu
            var = jnp.mean(d * d, axis=-1, keepdims=True)
            mu_ref[group(r), :] = jnp.broadcast_to(mu, (LN_ROWS, LANES))
            rstd_ref[group(r), :] = jnp.broadcast_to(lax.rsqrt(var + LN_EPS), (LN_ROWS, LANES))
            return carry

        lax.fori_loop(0, n_groups, stats, 0, unroll=LN_UNROLL)
        g = g_ref[...]
        b = b_ref[...]

        def normalise(r, carry):
            rows = group(r)
            out = (o32_ref[rows, :] - mu_ref[rows, :1]) * rstd_ref[rows, :1] * g + b
            o32_ref[rows, :] = out
            if o2_ref.dtype == jnp.uint32:
                o2_ref[rows, :] = _pack_bf16_pair(out[:, :half], out[:, half:])
            else:
                o2_ref[rows, :] = out.astype(o2_ref.dtype)
            return carry

        lax.fori_loop(0, n_groups, normalise, 0, unroll=LN_UNROLL)


def matmul_residual_ln(a, w, resid, g, b, *, tm, tk, packed, name):
    M, K = a.shape
    N = w.shape[1]
    nk = K // tk
    row_spec = pl.BlockSpec((tm, N), lambda i, k: (i, 0))
    vec_spec = pl.BlockSpec((1, N), lambda i, k: (0, 0))
    out_shape = [jax.ShapeDtypeStruct((M, N), F32)]
    out_specs = [row_spec]
    if packed:
        out_shape.append(jax.ShapeDtypeStruct((M, N // 2), jnp.uint32))
        out_specs.append(pl.BlockSpec((tm, N // 2), lambda i, k: (i, 0)))
    else:
        out_shape.append(jax.ShapeDtypeStruct((M, N), BF16))
        out_specs.append(row_spec)
    return pl.pallas_call(
        functools.partial(_mm_res_ln_kernel, nk=nk), grid=(M // tm, nk),
        in_specs=[pl.BlockSpec((tm, tk), lambda i, k: (i, k)),
                  pl.BlockSpec((tk, N), lambda i, k: (k, 0)),
                  pl.BlockSpec((tm, N), lambda i, k: (i, 0), pipeline_mode=pl.Buffered(1)),
                  vec_spec, vec_spec],
        out_specs=out_specs, out_shape=out_shape,
        scratch_shapes=[pltpu.VMEM((tm, LANES), F32), pltpu.VMEM((tm, LANES), F32)],
        compiler_params=_params(2), name=name)(a, w, resid, g.reshape(1, N), b.reshape(1, N))


def _shift_rows(cur, halo, s):
    rolled = pltpu.roll(cur, s, 0)
    halo_top = pltpu.roll(halo, s, 0)[:8]
    row = lax.broadcasted_iota(jnp.int32, (8, cur.shape[1]), 0)
    top = jnp.where(row < s, halo_top, rolled[:8])
    return jnp.concatenate([top, rolled[8:]], axis=0)


def _conv_qkv_kernel(cur_ref, halo_ref, cw_ref, cb_ref, wq_ref, wk_ref, wv_ref, wg_ref,
                     xc_ref, q_ref, k_ref, v_ref, gates_ref):
    i = pl.program_id(0)
    c = pl.program_id(1)
    cur_b = cur_ref[...]
    cur = cur_b.astype(F32)
    halo = jnp.where(i > 0, halo_ref[...].astype(F32), 0.0)
    cw = cw_ref[...]
    acc = cw[MLSTM_CONV_WIDTH - 1:MLSTM_CONV_WIDTH, :] * cur + cb_ref[...]
    for s in range(1, MLSTM_CONV_WIDTH):
        j = MLSTM_CONV_WIDTH - 1 - s
        acc = acc + cw[j:j + 1, :] * _shift_rows(cur, halo, s)
    xc_b = jax.nn.silu(acc).astype(BF16)
    xc_ref[...] = xc_b

    def block_diag(x_b, w_ref):
        n = x_b.shape[1] // MXU_DIM
        parts = [jnp.dot(x_b[:, MXU_DIM * j:MXU_DIM * (j + 1)], w_ref[j], preferred_element_type=F32)
                 for j in range(n)]
        return jnp.concatenate(parts, axis=1).astype(BF16)

    q_b = block_diag(xc_b, wq_ref)
    k_b = block_diag(xc_b, wk_ref)
    v_b = block_diag(cur_b, wv_ref)
    q_ref[...] = q_b
    k_ref[...] = k_b
    v_ref[...] = v_b
    part = (jnp.dot(q_b, wg_ref[0], preferred_element_type=F32)
            + jnp.dot(k_b, wg_ref[1], preferred_element_type=F32)
            + jnp.dot(v_b, wg_ref[2], preferred_element_type=F32))

    @pl.when(c == 0)
    def _():
        gates_ref[...] = jnp.zeros_like(gates_ref)

    gates_ref[...] += part


def _expand_block_diag(w):
    nb, blk, _ = w.shape
    per = MXU_DIM // blk
    wt = w.reshape(nb // per, per, blk, blk)
    eye = jnp.eye(per, dtype=w.dtype)
    full = jnp.einsum("tpcd,pq->tpcqd", wt, eye)
    return full.reshape(nb // per, MXU_DIM, MXU_DIM).astype(BF16)


def conv_qkv(xz, conv_w, conv_b, w_q, w_k, w_v, w_gates, *, tm, tc):
    S = xz.shape[0]
    C = conv_w.shape[1]
    ng = w_gates.shape[2]
    wg = jnp.zeros((3, C, LANES), BF16).at[:, :, :ng].set(w_gates.astype(BF16))
    tiles = tc // MXU_DIM
    halo_rows = BF16_SUBLANES
    blk = pl.BlockSpec((tm, tc), lambda i, c: (i, c))
    bd_spec = pl.BlockSpec((tiles, MXU_DIM, MXU_DIM), lambda i, c: (c, 0, 0))
    act = jax.ShapeDtypeStruct((S, C), BF16)
    return pl.pallas_call(
        _conv_qkv_kernel, grid=(S // tm, C // tc),
        in_specs=[blk,
                  pl.BlockSpec((halo_rows, tc), lambda i, c: (jnp.maximum(i * (tm // halo_rows) - 1, 0), c)),
                  pl.BlockSpec((MLSTM_CONV_WIDTH, tc), lambda i, c: (0, c)),
                  pl.BlockSpec((1, tc), lambda i, c: (0, c)),
                  bd_spec, bd_spec, bd_spec,
                  pl.BlockSpec((3, tc, LANES), lambda i, c: (0, c, 0))],
        out_specs=[blk, blk, blk, blk, pl.BlockSpec((tm, LANES), lambda i, c: (i, 0))],
        out_shape=[act, act, act, act, jax.ShapeDtypeStruct((S, LANES), F32)],
        compiler_params=_params(2), name="l0_conv_qkv")(
            xz, xz, conv_w, conv_b.reshape(1, C),
            _expand_block_diag(w_q), _expand_block_diag(w_k), _expand_block_diag(w_v), wg)


def _mlstm_kernel(q_ref, k_ref, v_ref, z_ref, xc_ref, gates_ref, gbias_ref, hng_ref, skip_ref,
                  o_ref, c_ref, cb_ref, n_ref, m_ref, hh_ref, kwt_ref, *, heads):
    heads_per_step = c_ref.shape[0]
    dh = q_ref.shape[1] // heads_per_step

    @pl.when(pl.program_id(1) == 0)
    def _():
        c_ref[...] = jnp.zeros_like(c_ref)
        cb_ref[...] = jnp.zeros_like(cb_ref)
        n_ref[...] = jnp.zeros_like(n_ref)
        m_ref[...] = jnp.zeros_like(m_ref)

    g = gates_ref[...] + gbias_ref[...]
    for i in range(heads_per_step):
        cols = slice(i * dh, (i + 1) * dh)
        _mlstm_head(pl.program_id(0) * heads_per_step + i, heads, g,
                    q_ref.at[:, cols], k_ref.at[:, cols], v_ref.at[:, cols], z_ref.at[:, cols], xc_ref.at[:, cols],
                    hng_ref.at[:, cols], skip_ref.at[:, cols], o_ref.at[:, cols],
                    c_ref.at[i], cb_ref.at[i], n_ref.at[i], m_ref.at[i], hh_ref.at[i], kwt_ref.at[i])


def _mlstm_head(h, heads, g, q_ref, k_ref, v_ref, z_ref, xc_ref, hng_ref, skip_ref,
                o_ref, c_ref, cb_ref, n_ref, m_ref, hh_ref, kwt_ref):
    L, dh = q_ref.shape
    scale = dh ** -0.5
    lane = lax.broadcasted_iota(jnp.int32, g.shape, 1)
    ig = jnp.sum(jnp.where(lane == h, g, 0.0), axis=1, keepdims=True)
    fpre = jnp.sum(jnp.where(lane == heads + h, g, 0.0), axis=1, keepdims=True)
    lf = jnp.minimum(fpre, 0.0) - jnp.log1p(jnp.exp(-jnp.abs(fpre)))

    row = lax.broadcasted_iota(jnp.int32, (L, L), 0)
    col = lax.broadcasted_iota(jnp.int32, (L, L), 1)
    causal = col <= row
    bcum = jnp.dot(causal.astype(F32), jnp.broadcast_to(lf, (L, LANES)),
                   precision=lax.Precision.HIGHEST, preferred_element_type=F32)[:, :1]
    r_row = jnp.transpose(jnp.broadcast_to(ig - bcum, (L, LANES)))[:1, :]

    m_prev = m_ref[:1, :1]
    dlog = jnp.where(causal, bcum + r_row, -jnp.inf)
    inter_log = bcum + m_prev
    m_t = jnp.maximum(inter_log, jnp.max(dlog, axis=1, keepdims=True))
    dw = jnp.exp(dlog - m_t)
    inter_w = jnp.exp(inter_log - m_t)

    qb = q_ref[...]
    kb = k_ref[...]
    scores = lax.dot_general(qb, kb, (((1,), (1,)), ((), ())), preferred_element_type=F32) * (dw * scale)
    scores_b = scores.astype(BF16)
    qn = jnp.sum(qb.astype(F32) * n_ref[:1, :], axis=1, keepdims=True)
    den = jnp.sum(scores, axis=1, keepdims=True) + inter_w * qn
    inv_den = 1.0 / jnp.maximum(jnp.abs(den), jnp.exp(-m_t))

    width = min(MXU_DIM, dh)
    col_blocks = [slice(j * width, (j + 1) * width) for j in range(dh // width)]
    row_sum = jnp.zeros((L, 1), F32)
    for cols in col_blocks:
        num = (jnp.dot(scores_b, v_ref[:, cols], preferred_element_type=F32)
               + inter_w * jnp.dot(qb, cb_ref[:, cols], preferred_element_type=F32))
        hblk = num * inv_den
        hh_ref[:, cols] = hblk
        row_sum = row_sum + jnp.sum(hblk, axis=1, keepdims=True)

    b_last = bcum[L - 1:L, :]
    wlog = b_last - bcum + ig
    m_new = jnp.maximum(b_last + m_prev, jnp.max(wlog, axis=0, keepdims=True))
    ws = jnp.exp(wlog - m_new)
    cw = jnp.exp(b_last + m_prev - m_new)
    kw = kb.astype(F32) * (ws * scale)
    n_ref[...] = jnp.broadcast_to(cw * n_ref[:1, :] + jnp.sum(kw, axis=0, keepdims=True), n_ref.shape)
    m_ref[...] = jnp.broadcast_to(m_new, m_ref.shape)
    kwt_ref[...] = jnp.transpose(kw).astype(BF16)
    for cols in col_blocks:
        c_new = cw * c_ref[:, cols] + jnp.dot(kwt_ref[...], v_ref[:, cols], preferred_element_type=F32)
        c_ref[:, cols] = c_new
        cb_ref[:, cols] = c_new.astype(BF16)

    mu = row_sum * (1.0 / dh)
    sq_sum = jnp.zeros((L, 1), F32)
    for cols in col_blocks:
        d = hh_ref[:, cols] - mu
        sq_sum = sq_sum + jnp.sum(d * d, axis=1, keepdims=True)
    rstd = lax.rsqrt(sq_sum * (1.0 / dh) + LN_EPS)
    for cols in col_blocks:
        hn = (hh_ref[:, cols] - mu) * rstd * hng_ref[:, cols]
        out = jax.nn.sigmoid(z_ref[:, cols].astype(F32)) * (hn + skip_ref[:, cols] * xc_ref[:, cols].astype(F32))
        o_ref[:, cols] = out.astype(o_ref.dtype)


def mlstm(q, k, v, xz, xc, gates, b_igate, b_fgate, head_norm_g, skip, *, chunk, heads_per_step):
    S, C = q.shape
    heads = b_igate.shape[0]
    dh = C // heads
    hp = heads_per_step
    gbias = jnp.zeros((1, LANES), F32).at[0, :heads].set(b_igate).at[0, heads:2 * heads].set(b_fgate)
    blk = pl.BlockSpec((chunk, hp * dh), lambda p, c: (c, p))
    vec = pl.BlockSpec((1, hp * dh), lambda p, c: (0, p))
    return pl.pallas_call(
        functools.partial(_mlstm_kernel, heads=heads), grid=(heads // hp, S // chunk),
        in_specs=[blk, blk, blk,
                  pl.BlockSpec((chunk, hp * dh), lambda p, c: (c, heads // hp + p)),
                  blk,
                  pl.BlockSpec((chunk, LANES), lambda p, c: (c, 0)),
                  pl.BlockSpec((1, LANES), lambda p, c: (0, 0)),
                  vec, vec],
        out_specs=blk,
        out_shape=jax.ShapeDtypeStruct((S, C), BF16),
        scratch_shapes=[pltpu.VMEM((hp, dh, dh), F32), pltpu.VMEM((hp, dh, dh), BF16),
                        pltpu.VMEM((hp, 8, dh), F32), pltpu.VMEM((hp, 8, LANES), F32),
                        pltpu.VMEM((hp, chunk, dh), F32), pltpu.VMEM((hp, dh, chunk), BF16)],
        compiler_params=_params(2), name="l0_mlstm")(
            q, k, v, xz, xc, gates, gbias, head_norm_g.reshape(1, C), skip.reshape(1, C))


def _spatial_gate_kernel(u_ref, v_ref, ng_ref, nb_ref, ws_ref, bs_ref, o_ref, *, groups):
    T = v_ref.shape[0]
    gd = v_ref.shape[1] // groups
    vn = _layer_norm_rows(v_ref[...].astype(F32), ng_ref[...], nb_ref[...]).astype(BF16)
    row = lax.broadcasted_iota(jnp.int32, (T, T), 0)
    col = lax.broadcasted_iota(jnp.int32, (T, T), 1)
    causal = col <= row
    bs = bs_ref[...]
    for g in range(groups):
        wc = jnp.where(causal, ws_ref[g], 0.0).astype(BF16)
        sv = jnp.dot(wc, vn[:, g * gd:(g + 1) * gd], preferred_element_type=F32) + bs[:, g:g + 1]
        o_ref[:, g * gd:(g + 1) * gd] = (u_ref[:, g * gd:(g + 1) * gd].astype(F32) * sv).astype(o_ref.dtype)


def spatial_gate(uv, norm_g, norm_b, w_s, b_s):
    S = uv.shape[0]
    W = norm_g.shape[0]
    groups, T, _ = w_s.shape
    bs_cols = jnp.zeros((T, LANES), F32).at[:, :groups].set(b_s.T)
    return pl.pallas_call(
        functools.partial(_spatial_gate_kernel, groups=groups), grid=(S // T,),
        in_specs=[pl.BlockSpec((T, W), lambda c: (c, 0)),
                  pl.BlockSpec((T, W), lambda c: (c, 1)),
                  pl.BlockSpec((1, W), lambda c: (0, 0)),
                  pl.BlockSpec((1, W), lambda c: (0, 0)),
                  pl.BlockSpec((groups, T, T), lambda c: (0, 0, 0)),
                  pl.BlockSpec((T, LANES), lambda c: (0, 0))],
        out_specs=pl.BlockSpec((T, W), lambda c: (c, 0)),
        out_shape=jax.ShapeDtypeStruct((S, W), BF16),
        compiler_params=_params(1), name="l1_spatial_gate")(
            uv, uv, norm_g.reshape(1, W), norm_b.reshape(1, W), w_s, bs_cols)


INFO_IDX0, INFO_IDX1, INFO_GATE0, INFO_GATE1, INFO_RANK0, INFO_RANK1 = range(6)


def _router_kernel(x_ref, w_ref, b_ref, info_ref, count_ref, carry_ref, *, n_experts):
    i = pl.program_id(0)
    tm = x_ref.shape[0]

    @pl.when(i == 0)
    def _():
        carry_ref[...] = jnp.zeros_like(carry_ref)

    logits = jnp.dot(x_ref[...], w_ref[...], precision=lax.Precision.HIGHEST,
                     preferred_element_type=F32) + b_ref[...]
    lane = lax.broadcasted_iota(jnp.int32, logits.shape, 1).astype(F32)
    neg_inf = -jnp.inf
    logits = jnp.where(lane < n_experts, logits, neg_inf)
    m0 = jnp.max(logits, axis=1, keepdims=True)
    i0 = jnp.min(jnp.where(logits == m0, lane, float(LANES)), axis=1, keepdims=True)
    rest = jnp.where(lane == i0, neg_inf, logits)
    m1 = jnp.max(rest, axis=1, keepdims=True)
    i1 = jnp.min(jnp.where(rest == m1, lane, float(LANES)), axis=1, keepdims=True)
    e1 = jnp.exp(m1 - m0)
    denom = 1.0 + e1
    g0 = 1.0 / denom
    g1 = e1 / denom
    hot0 = lane == i0
    hot1 = lane == i1
    member = jnp.where(hot0 | hot1, 1.0, 0.0)
    row = lax.broadcasted_iota(jnp.int32, (tm, tm), 0)
    col = lax.broadcasted_iota(jnp.int32, (tm, tm), 1)
    before = jnp.where(col < row, 1.0, 0.0).astype(BF16)
    rank = jnp.dot(before, member.astype(BF16), preferred_element_type=F32) + carry_ref[:1, :]
    r0 = jnp.sum(jnp.where(hot0, rank, 0.0), axis=1, keepdims=True)
    r1 = jnp.sum(jnp.where(hot1, rank, 0.0), axis=1, keepdims=True)
    info = jnp.zeros_like(logits)
    for slot, val in ((INFO_IDX0, i0), (INFO_IDX1, i1), (INFO_GATE0, g0), (INFO_GATE1, g1),
                      (INFO_RANK0, r0), (INFO_RANK1, r1)):
        info = jnp.where(lane == slot, val, info)
    info_ref[...] = info
    total = carry_ref[...] + jnp.sum(member, axis=0, keepdims=True)
    carry_ref[...] = total
    count_ref[...] = total


def router(x, router_w, router_b, *, tm):
    S, D = x.shape
    E = router_w.shape[1]
    w = jnp.zeros((D, LANES), F32).at[:, :E].set(router_w)
    b = jnp.zeros((1, LANES), F32).at[0, :E].set(router_b)
    return pl.pallas_call(
        functools.partial(_router_kernel, n_experts=E), grid=(S // tm,),
        in_specs=[pl.BlockSpec((tm, D), lambda i: (i, 0)),
                  pl.BlockSpec((D, LANES), lambda i: (0, 0)),
                  pl.BlockSpec((1, LANES), lambda i: (0, 0))],
        out_specs=[pl.BlockSpec((tm, LANES), lambda i: (i, 0)),
                   pl.BlockSpec((8, LANES), lambda i: (0, 0))],
        out_shape=[jax.ShapeDtypeStruct((S, LANES), F32), jax.ShapeDtypeStruct((8, LANES), F32)],
        scratch_shapes=[pltpu.VMEM((8, LANES), F32)],
        compiler_params=_params(1), name="l1_router")(x, w, b)


def _row_copy(src_hbm, dst_vmem, sem, src_row, dst_row):
    return pltpu.make_async_copy(src_hbm.at[pl.ds(src_row, 1), :], dst_vmem.at[pl.ds(dst_row, 1), :], sem)


def _gather_rows_kernel(row_src_ref, n_used_ref, x_hbm, o_ref, buf_ref, sem):
    t = pl.program_id(0)
    tm = o_ref.shape[0]
    n_used = n_used_ref[0]

    def start_tile(tile, slot):
        def start(r, carry):
            _row_copy(x_hbm, buf_ref.at[slot], sem.at[slot], row_src_ref[tile * tm + r], r).start()
            return carry
        lax.fori_loop(0, tm, start, 0)

    @pl.when(t == 0)
    def _():
        start_tile(0, 0)

    @pl.when(t + 1 < n_used)
    def _():
        start_tile(t + 1, (t + 1) % 2)

    @pl.when(t < n_used)
    def _():
        slot = t % 2

        def wait(r, carry):
            _row_copy(x_hbm, buf_ref.at[slot], sem.at[slot], 0, r).wait()
            return carry
        lax.fori_loop(0, tm, wait, 0)
        o_ref[...] = buf_ref[slot]

    @pl.when(t >= n_used)
    def _():
        o_ref[...] = jnp.zeros_like(o_ref)


def gather_rows(x, row_src, n_used, *, tm):
    P = row_src.shape[0]
    D = x.shape[1]
    grid_spec = pltpu.PrefetchScalarGridSpec(
        num_scalar_prefetch=2, grid=(P // tm,),
        in_specs=[pl.BlockSpec(memory_space=pl.ANY)],
        out_specs=pl.BlockSpec((tm, D), lambda t, rs, nu: (t, 0)),
        scratch_shapes=[pltpu.VMEM((2, tm, D), x.dtype), pltpu.SemaphoreType.DMA((2,))])
    return pl.pallas_call(
        _gather_rows_kernel, grid_spec=grid_spec,
        out_shape=jax.ShapeDtypeStruct((P, D), x.dtype),
        compiler_params=_params(1), name="l1_moe_gather")(row_src, n_used, x)


def _expert_up_tile(xp_ref, w1_ref, w3_ref, o_ref):
    lo, hi = _unpack_bf16_pair(xp_ref[...])
    lo = lo.astype(BF16)
    hi = hi.astype(BF16)
    half = xp_ref.shape[1]

    def mm(w_ref):
        return (jnp.dot(lo, w_ref[:half, :], preferred_element_type=F32)
                + jnp.dot(hi, w_ref[half:, :], preferred_element_type=F32))

    o_ref[...] = (jax.nn.silu(mm(w1_ref)) * mm(w3_ref)).astype(o_ref.dtype)


def _expert_down_tile(h_ref, w_ref, o_ref):
    y = jnp.dot(h_ref[...], w_ref[...], preferred_element_type=F32)
    half = y.shape[1] // 2
    o_ref[...] = _pack_bf16_pair(y[:, :half], y[:, half:])


def _grouped_kernel(te_ref, n_used_ref, x_ref, *refs, n_weights, tile_body):
    w_refs = refs[:n_weights]
    o_ref = refs[n_weights]
    wb_refs = refs[n_weights + 1:]
    t = pl.program_id(1)
    used = t < n_used_ref[0]
    new_weights = jnp.logical_or(t == 0, te_ref[t] != te_ref[jnp.maximum(t - 1, 0)])

    @pl.when(jnp.logical_and(used, new_weights))
    def _():
        for w_ref, wb_ref in zip(w_refs, wb_refs):
            wb_ref[...] = w_ref[...].astype(BF16)

    @pl.when(used)
    def _():
        tile_body(x_ref, *wb_refs, o_ref)

    @pl.when(jnp.logical_not(used))
    def _():
        o_ref[...] = jnp.zeros_like(o_ref)


def _grouped_call(tile_body, x, weights, tile_expert, n_used, *, tm, tn, out_cols, out_dtype, name):
    P, xcols = x.shape
    _, K, N = weights[0].shape
    n_col_tiles = N // tn

    def row_map(j, t, te, nu):
        return (jnp.minimum(t, nu[0] - 1), 0)

    def out_map(j, t, te, nu):
        return (t, j)

    def w_map(j, t, te, nu):
        return (te[t], 0, j)

    grid_spec = pltpu.PrefetchScalarGridSpec(
        num_scalar_prefetch=2, grid=(n_col_tiles, P // tm),
        in_specs=[pl.BlockSpec((tm, xcols), row_map)] + [pl.BlockSpec((None, K, tn), w_map)] * len(weights),
        out_specs=pl.BlockSpec((tm, out_cols // n_col_tiles), out_map),
        scratch_shapes=[pltpu.VMEM((K, tn), BF16)] * len(weights))
    return pl.pallas_call(
        functools.partial(_grouped_kernel, n_weights=len(weights), tile_body=tile_body), grid_spec=grid_spec,
        out_shape=jax.ShapeDtypeStruct((P, out_cols), out_dtype),
        compiler_params=_params(2), name=name)(tile_expert, n_used, x, *weights)


def _combine_ln_kernel(pos0_ref, pos1_ref, y_hbm, x_ref, info_ref, g_ref, b_ref, o_ref,
                       buf0_ref, buf1_ref, sem, *, col_tiles):
    i = pl.program_id(0)
    tm = x_ref.shape[0]

    def start(r, carry):
        _row_copy(y_hbm, buf0_ref, sem, pos0_ref[i * tm + r], r).start()
        _row_copy(y_hbm, buf1_ref, sem, pos1_ref[i * tm + r], r).start()
        return carry

    def wait(r, carry):
        _row_copy(y_hbm, buf0_ref, sem, 0, r).wait()
        _row_copy(y_hbm, buf1_ref, sem, 0, r).wait()
        return carry

    lax.fori_loop(0, tm, start, 0)
    lax.fori_loop(0, tm, wait, 0)
    info = info_ref[...]
    g0 = info[:, INFO_GATE0:INFO_GATE0 + 1]
    g1 = info[:, INFO_GATE1:INFO_GATE1 + 1]
    lo0, hi0 = _unpack_bf16_pair(buf0_ref[...])
    lo1, hi1 = _unpack_bf16_pair(buf1_ref[...])
    lo = g0 * lo0 + g1 * lo1
    hi = g0 * hi0 + g1 * hi1
    w = lo.shape[1] // col_tiles
    y = jnp.concatenate([part[:, j * w:(j + 1) * w] for j in range(col_tiles) for part in (lo, hi)], axis=1)
    o_ref[...] = _layer_norm_rows(ALPHA * x_ref[...] + y, g_ref[...], b_ref[...])


def combine_ln(y_rows, pos0, pos1, info, x, g, b, *, tm, col_tiles):
    S, D = x.shape
    half = y_rows.shape[1]
    grid_spec = pltpu.PrefetchScalarGridSpec(
        num_scalar_prefetch=2, grid=(S // tm,),
        in_specs=[pl.BlockSpec(memory_space=pl.ANY),
                  pl.BlockSpec((tm, D), lambda i, p0, p1: (i, 0)),
                  pl.BlockSpec((tm, LANES), lambda i, p0, p1: (i, 0)),
                  pl.BlockSpec((1, D), lambda i, p0, p1: (0, 0)),
                  pl.BlockSpec((1, D), lambda i, p0, p1: (0, 0))],
        out_specs=pl.BlockSpec((tm, D), lambda i, p0, p1: (i, 0)),
        scratch_shapes=[pltpu.VMEM((tm, half), jnp.uint32), pltpu.VMEM((tm, half), jnp.uint32),
                        pltpu.SemaphoreType.DMA(())])
    return pl.pallas_call(
        functools.partial(_combine_ln_kernel, col_tiles=col_tiles), grid_spec=grid_spec,
        out_shape=jax.ShapeDtypeStruct((S, D), F32),
        compiler_params=_params(1), name="l1_moe_combine_ln")(
            pos0, pos1, y_rows, x, info, g.reshape(1, D), b.reshape(1, D))


def moe_layer(x, xp, router_w, router_b, w1, w3, w2, ln_g, ln_b, *, tm_route, tm, tn_up, tn_down, tm_combine):
    S, D = x.shape
    E = router_w.shape[1]
    info, counts = router(x, router_w, router_b, tm=tm_route)
    idx0 = info[:, INFO_IDX0].astype(jnp.int32)
    idx1 = info[:, INFO_IDX1].astype(jnp.int32)
    count = counts[0, :E].astype(jnp.int32)
    tiles = (count + tm - 1) // tm
    tile_end = jnp.cumsum(tiles)
    offset = (tile_end - tiles) * tm
    n_used = tile_end[-1:]
    pos0 = offset[idx0] + info[:, INFO_RANK0].astype(jnp.int32)
    pos1 = offset[idx1] + info[:, INFO_RANK1].astype(jnp.int32)
    n_rows = TOP_K * S + E * tm
    n_tiles = n_rows // tm
    tile_id = jnp.minimum(jnp.arange(n_tiles, dtype=jnp.int32), n_used[0] - 1)
    tile_expert = jnp.sum(tile_id[:, None] >= tile_end[None, :], axis=1).astype(jnp.int32)
    token = jnp.arange(S, dtype=jnp.int32)
    row_src = jnp.zeros((n_rows,), jnp.int32).at[pos0].set(token).at[pos1].set(token)

    xs = gather_rows(xp, row_src, n_used, tm=tm)
    hs = _grouped_call(_expert_up_tile, xs, (w1, w3), tile_expert, n_used,
                       tm=tm, tn=tn_up, out_cols=w1.shape[2], out_dtype=BF16, name="l1_moe_up")
    ys = _grouped_call(_expert_down_tile, hs, (w2,), tile_expert, n_used,
                       tm=tm, tn=tn_down, out_cols=D // 2, out_dtype=jnp.uint32, name="l1_moe_down")
    return combine_ln(ys, pos0, pos1, info, x, ln_g, ln_b, tm=tm_combine, col_tiles=D // tn_down)


def kernel(x, l0_mix_w_in, l0_conv_w, l0_conv_b, l0_w_q, l0_w_k, l0_w_v, l0_w_gates, l0_b_igate, l0_b_fgate, l0_head_norm_g, l0_skip, l0_mix_w_out, l0_ln1_g, l0_ln1_b, l0_ffn_w1, l0_ffn_w3, l0_ffn_w2, l0_ln2_g, l0_ln2_b, l1_mix_w_in, l1_mix_b_in, l1_sg_norm_g, l1_sg_norm_b, l1_sg_w, l1_sg_b, l1_mix_w_out, l1_ln1_g, l1_ln1_b, l1_router_w, l1_router_b, l1_exp_w1, l1_exp_w3, l1_exp_w2, l1_ln2_g, l1_ln2_b):
    B, S, D = x.shape
    x0 = x.reshape(B * S, D)

    def bf(w):
        return w.astype(BF16)

    tm_mm = min(MM_ROW_TILE, S)
    tm_ln = min(LN_ROW_TILE, S)
    xz = matmul(bf(x0), l0_mix_w_in, tm=tm_mm, tn=MM_COL_TILE, name="l0_in_proj")
    xc, q, k, v, gates = conv_qkv(xz, l0_conv_w, l0_conv_b, l0_w_q, l0_w_k, l0_w_v, l0_w_gates,
                                  tm=min(CONV_ROW_TILE, S), tc=min(CONV_CHANNEL_TILE, l0_conv_w.shape[1]))
    hg = mlstm(q, k, v, xz, xc, gates, l0_b_igate, l0_b_fgate, l0_head_norm_g, l0_skip,
               chunk=MLSTM_KERNEL_CHUNK, heads_per_step=MLSTM_HEADS_PER_STEP)
    x1, x1b = matmul_residual_ln(hg, bf(l0_mix_w_out), x0, l0_ln1_g, l0_ln1_b,
                                 tm=tm_ln, tk=LN_K_TILE, packed=False, name="l0_out_proj_ln")
    h = matmul_swiglu(x1b, l0_ffn_w1, l0_ffn_w3, tm=tm_mm, tn=SWIGLU_COL_TILE, name="l0_ffn_up")
    x2, x2b = matmul_residual_ln(h, bf(l0_ffn_w2), x1, l0_ln2_g, l0_ln2_b,
                                 tm=tm_ln, tk=LN_K_TILE, packed=False, name="l0_ffn_down_ln")
    uv = matmul(x2b, l1_mix_w_in, l1_mix_b_in, tm=tm_mm, tn=MM_COL_TILE, name="l1_in_proj_gelu")
    gated = spatial_gate(uv, l1_sg_norm_g, l1_sg_norm_b, l1_sg_w, l1_sg_b)
    x3, x3p = matmul_residual_ln(gated, bf(l1_mix_w_out), x2, l1_ln1_g, l1_ln1_b,
                                 tm=tm_ln, tk=LN_K_TILE, packed=True, name="l1_out_proj_ln")
    y = moe_layer(x3, x3p, l1_router_w, l1_router_b, l1_exp_w1, l1_exp_w3, l1_exp_w2, l1_ln2_g, l1_ln2_b,
                  tm_route=min(ROUTER_ROW_TILE, S), tm=EXPERT_ROW_TILE,
                  tn_up=min(EXPERT_UP_COL_TILE, l1_exp_w1.shape[2]), tn_down=min(EXPERT_DOWN_COL_TILE, D),
                  tm_combine=min(COMBINE_ROW_TILE, S))
    return y.reshape(B, S, D)
```

```python
import functools

import jax
import jax.numpy as jnp
from jax import lax
from jax.experimental import pallas as pl
from jax.experimental.pallas import tpu as pltpu

F32 = jnp.float32
BF16 = jnp.bfloat16

MLSTM_HEADS = 8
MLSTM_QKV_BLOCK = 4
MLSTM_CONV_WIDTH = 4
SG_CHUNK = 128
SG_GROUPS = 8
N_EXPERTS = 8
TOP_K = 2
DEPTH = 2
ALPHA = (2 * DEPTH) ** 0.25
LN_EPS = 1e-5

LANES = 128
BF16_SUBLANES = 16
MXU_DIM = 256
VMEM_LIMIT_BYTES = 56 * 1024 * 1024

MLSTM_KERNEL_CHUNK = 256
MLSTM_HEADS_PER_STEP = 2

MM_ROW_TILE = 1024
MM_COL_TILE = 512
SWIGLU_COL_TILE = 256
LN_ROW_TILE = 512
LN_K_TILE = 1792
CONV_ROW_TILE = 512
CONV_CHANNEL_TILE = 1024
ROUTER_ROW_TILE = 512
EXPERT_ROW_TILE = 512
EXPERT_UP_COL_TILE = 512
EXPERT_DOWN_COL_TILE = 1024
COMBINE_ROW_TILE = 256


def _params(n_axes):
    return pltpu.CompilerParams(dimension_semantics=("arbitrary",) * n_axes,
                                vmem_limit_bytes=VMEM_LIMIT_BYTES)


def _layer_norm_rows(y, g, b):
    mu = jnp.mean(y, axis=-1, keepdims=True)
    d = y - mu
    var = jnp.mean(d * d, axis=-1, keepdims=True)
    return d * lax.rsqrt(var + LN_EPS) * g + b


def _pack_bf16_pair(lo, hi):
    lo_bits = lax.bitcast_convert_type(lo.astype(BF16).astype(F32), jnp.uint32) >> 16
    hi_bits = lax.bitcast_convert_type(hi.astype(BF16).astype(F32), jnp.uint32) & jnp.uint32(0xFFFF0000)
    return hi_bits | lo_bits


def _unpack_bf16_pair(p):
    lo = lax.bitcast_convert_type(p << 16, F32)
    hi = lax.bitcast_convert_type(p & jnp.uint32(0xFFFF0000), F32)
    return lo, hi


def _mm_kernel(x_ref, w_ref, o_ref):
    o_ref[...] = jnp.dot(x_ref[...], w_ref[...].astype(BF16), preferred_element_type=F32).astype(o_ref.dtype)


def _mm_bias_gelu_kernel(x_ref, w_ref, b_ref, o_ref):
    y = jnp.dot(x_ref[...], w_ref[...].astype(BF16), preferred_element_type=F32) + b_ref[...]
    o_ref[...] = jax.nn.gelu(y).astype(o_ref.dtype)


def matmul(x, w, bias=None, *, tm, tn, name):
    M, K = x.shape
    N = w.shape[1]
    in_specs = [pl.BlockSpec((tm, K), lambda i, j: (i, 0)),
                pl.BlockSpec((K, tn), lambda i, j: (0, j))]
    args = [x, w]
    body = _mm_kernel
    if bias is not None:
        in_specs.append(pl.BlockSpec((1, tn), lambda i, j: (0, j)))
        args.append(bias.reshape(1, N).astype(F32))
        body = _mm_bias_gelu_kernel
    return pl.pallas_call(
        body, grid=(M // tm, N // tn), in_specs=in_specs,
        out_specs=pl.BlockSpec((tm, tn), lambda i, j: (i, j)),
        out_shape=jax.ShapeDtypeStruct((M, N), BF16),
        compiler_params=_params(2), name=name)(*args)


def _mm_swiglu_kernel(x_ref, w1_ref, w3_ref, o_ref):
    x = x_ref[...]
    a = jnp.dot(x, w1_ref[...].astype(BF16), preferred_element_type=F32)
    b = jnp.dot(x, w3_ref[...].astype(BF16), preferred_element_type=F32)
    o_ref[...] = (jax.nn.silu(a) * b).astype(o_ref.dtype)


def matmul_swiglu(x, w1, w3, *, tm, tn, name):
    M, K = x.shape
    N = w1.shape[1]
    wspec = pl.BlockSpec((K, tn), lambda i, j: (0, j))
    return pl.pallas_call(
        _mm_swiglu_kernel, grid=(M // tm, N // tn),
        in_specs=[pl.BlockSpec((tm, K), lambda i, j: (i, 0)), wspec, wspec],
        out_specs=pl.BlockSpec((tm, tn), lambda i, j: (i, j)),
        out_shape=jax.ShapeDtypeStruct((M, N), BF16),
        compiler_params=_params(2), name=name)(x, w1, w3)


def _mm_res_ln_kernel(a_ref, w_ref, r_ref, g_ref, b_ref, o32_ref, o2_ref, acc0_ref, acc1_ref, *, n_row_blocks):
    i = pl.program_id(0)
    k = pl.program_id(1)
    rc, n = o32_ref.shape

    @pl.when(jnp.logical_and(i == 0, k == 0))
    def _():
        acc0_ref[...] = jnp.zeros_like(acc0_ref)
        acc1_ref[...] = jnp.zeros_like(acc1_ref)

    def finish_slice(done_ref):
        rows = pl.ds(pl.multiple_of(k * rc, rc), rc)
        y = ALPHA * r_ref[...] + done_ref[rows, :]
        done_ref[rows, :] = jnp.zeros((rc, n), F32)
        out = _layer_norm_rows(y, g_ref[...], b_ref[...])
        o32_ref[...] = out
        if o2_ref.dtype == jnp.uint32:
            o2_ref[...] = _pack_bf16_pair(out[:, :n // 2], out[:, n // 2:])
        else:
            o2_ref[...] = out.astype(o2_ref.dtype)

    def accumulate(acc_ref):
        acc_ref[...] += jnp.dot(a_ref[...], w_ref[...], preferred_element_type=F32)

    accs = (acc0_ref, acc1_ref)
    interior = jnp.logical_and(i > 0, i < n_row_blocks)

    @pl.when(i == 0)
    def _():
        accumulate(acc0_ref)

    for parity in range(2):
        @pl.when(jnp.logical_and(interior, i % 2 == parity))
        def _():
            finish_slice(accs[1 - parity])
            accumulate(accs[parity])

    @pl.when(i == n_row_blocks)
    def _():
        finish_slice(accs[(n_row_blocks - 1) % 2])


def _ln_k_steps(K, tm, max_tk):
    for nk in (1, 2, 4, 8, 16, 32, 64):
        if K % nk == 0 and (K // nk) % MXU_DIM == 0 and K // nk <= max_tk and tm % (8 * nk) == 0:
            return nk
    raise ValueError(f"no K tiling for K={K}, tm={tm}")


def matmul_residual_ln(a, w, resid, g, b, *, tm, max_tk, packed, name):
    M, K = a.shape
    N = w.shape[1]
    nk = _ln_k_steps(K, tm, max_tk)
    tk = K // nk
    rc = tm // nk
    n_row_blocks = M // tm
    last = n_row_blocks - 1

    def a_map(i, k):
        return (jnp.minimum(i, last), jnp.where(i <= last, k, nk - 1))

    def w_map(i, k):
        return (jnp.where(i <= last, k, nk - 1), 0)

    def slice_map(i, k):
        return (jnp.where(i == 0, 0, (i - 1) * nk + k), 0)

    vec_spec = pl.BlockSpec((1, N), lambda i, k: (0, 0))
    out_shape = [jax.ShapeDtypeStruct((M, N), F32)]
    out_specs = [pl.BlockSpec((rc, N), slice_map)]
    if packed:
        out_shape.append(jax.ShapeDtypeStruct((M, N // 2), jnp.uint32))
        out_specs.append(pl.BlockSpec((rc, N // 2), slice_map))
    else:
        out_shape.append(jax.ShapeDtypeStruct((M, N), BF16))
        out_specs.append(pl.BlockSpec((rc, N), slice_map))
    return pl.pallas_call(
        functools.partial(_mm_res_ln_kernel, n_row_blocks=n_row_blocks), grid=(n_row_blocks + 1, nk),
        in_specs=[pl.BlockSpec((tm, tk), a_map), pl.BlockSpec((tk, N), w_map),
                  pl.BlockSpec((rc, N), slice_map), vec_spec, vec_spec],
        out_specs=out_specs, out_shape=out_shape,
        scratch_shapes=[pltpu.VMEM((tm, N), F32), pltpu.VMEM((tm, N), F32)],
        compiler_params=_params(2), name=name)(a, w, resid, g.reshape(1, N), b.reshape(1, N))


def _shift_rows(cur, halo, s):
    rolled = pltpu.roll(cur, s, 0)
    halo_top = pltpu.roll(halo, s, 0)[:8]
    row = lax.broadcasted_iota(jnp.int32, (8, cur.shape[1]), 0)
    top = jnp.where(row < s, halo_top, rolled[:8])
    return jnp.concatenate([top, rolled[8:]], axis=0)


def _conv_qkv_kernel(cur_ref, halo_ref, cw_ref, cb_ref, wq_ref, wk_ref, wv_ref, wg_ref,
                     xc_ref, q_ref, k_ref, v_ref, gates_ref):
    i = pl.program_id(0)
    c = pl.program_id(1)
    cur_b = cur_ref[...]
    cur = cur_b.astype(F32)
    halo = jnp.where(i > 0, halo_ref[...].astype(F32), 0.0)
    cw = cw_ref[...]
    acc = cw[MLSTM_CONV_WIDTH - 1:MLSTM_CONV_WIDTH, :] * cur + cb_ref[...]
    for s in range(1, MLSTM_CONV_WIDTH):
        j = MLSTM_CONV_WIDTH - 1 - s
        acc = acc + cw[j:j + 1, :] * _shift_rows(cur, halo, s)
    xc_b = jax.nn.silu(acc).astype(BF16)
    xc_ref[...] = xc_b

    def block_diag(x_b, w_ref):
        n = x_b.shape[1] // MXU_DIM
        parts = [jnp.dot(x_b[:, MXU_DIM * j:MXU_DIM * (j + 1)], w_ref[j], preferred_element_type=F32)
                 for j in range(n)]
        return jnp.concatenate(parts, axis=1).astype(BF16)

    q_b = block_diag(xc_b, wq_ref)
    k_b = block_diag(xc_b, wk_ref)
    v_b = block_diag(cur_b, wv_ref)
    q_ref[...] = q_b
    k_ref[...] = k_b
    v_ref[...] = v_b
    part = (jnp.dot(q_b, wg_ref[0], preferred_element_type=F32)
            + jnp.dot(k_b, wg_ref[1], preferred_element_type=F32)
            + jnp.dot(v_b, wg_ref[2], preferred_element_type=F32))

    @pl.when(c == 0)
    def _():
        gates_ref[...] = jnp.zeros_like(gates_ref)

    gates_ref[...] += part


def _expand_block_diag(w):
    nb, blk, _ = w.shape
    per = MXU_DIM // blk
    wt = w.reshape(nb // per, per, blk, blk)
    eye = jnp.eye(per, dtype=w.dtype)
    full = jnp.einsum("tpcd,pq->tpcqd", wt, eye)
    return full.reshape(nb // per, MXU_DIM, MXU_DIM).astype(BF16)


def conv_qkv(xz, conv_w, conv_b, w_q, w_k, w_v, w_gates, *, tm, tc):
    S = xz.shape[0]
    C = conv_w.shape[1]
    ng = w_gates.shape[2]
    wg = jnp.zeros((3, C, LANES), BF16).at[:, :, :ng].set(w_gates.astype(BF16))
    tiles = tc // MXU_DIM
    halo_rows = BF16_SUBLANES
    blk = pl.BlockSpec((tm, tc), lambda i, c: (i, c))
    bd_spec = pl.BlockSpec((tiles, MXU_DIM, MXU_DIM), lambda i, c: (c, 0, 0))
    act = jax.ShapeDtypeStruct((S, C), BF16)
    return pl.pallas_call(
        _conv_qkv_kernel, grid=(S // tm, C // tc),
        in_specs=[blk,
                  pl.BlockSpec((halo_rows, tc), lambda i, c: (jnp.maximum(i * (tm // halo_rows) - 1, 0), c)),
                  pl.BlockSpec((MLSTM_CONV_WIDTH, tc), lambda i, c: (0, c)),
                  pl.BlockSpec((1, tc), lambda i, c: (0, c)),
                  bd_spec, bd_spec, bd_spec,
                  pl.BlockSpec((3, tc, LANES), lambda i, c: (0, c, 0))],
        out_specs=[blk, blk, blk, blk, pl.BlockSpec((tm, LANES), lambda i, c: (i, 0))],
        out_shape=[act, act, act, act, jax.ShapeDtypeStruct((S, LANES), F32)],
        compiler_params=_params(2), name="l0_conv_qkv")(
            xz, xz, conv_w, conv_b.reshape(1, C),
            _expand_block_diag(w_q), _expand_block_diag(w_k), _expand_block_diag(w_v), wg)


def _mlstm_kernel(q_ref, k_ref, v_ref, z_ref, xc_ref, gates_ref, gbias_ref, hng_ref, skip_ref,
                  o_ref, c_ref, cb_ref, n_ref, m_ref, hh_ref, kwt_ref, *, heads):
    heads_per_step = c_ref.shape[0]
    dh = q_ref.shape[1] // heads_per_step

    @pl.when(pl.program_id(1) == 0)
    def _():
        c_ref[...] = jnp.zeros_like(c_ref)
        cb_ref[...] = jnp.zeros_like(cb_ref)
        n_ref[...] = jnp.zeros_like(n_ref)
        m_ref[...] = jnp.zeros_like(m_ref)

    g = gates_ref[...] + gbias_ref[...]
    for i in range(heads_per_step):
        cols = slice(i * dh, (i + 1) * dh)
        _mlstm_head(pl.program_id(0) * heads_per_step + i, heads, g,
                    q_ref.at[:, cols], k_ref.at[:, cols], v_ref.at[:, cols], z_ref.at[:, cols], xc_ref.at[:, cols],
                    hng_ref.at[:, cols], skip_ref.at[:, cols], o_ref.at[:, cols],
                    c_ref.at[i], cb_ref.at[i], n_ref.at[i], m_ref.at[i], hh_ref.at[i], kwt_ref.at[i])


def _mlstm_head(h, heads, g, q_ref, k_ref, v_ref, z_ref, xc_ref, hng_ref, skip_ref,
                o_ref, c_ref, cb_ref, n_ref, m_ref, hh_ref, kwt_ref):
    L, dh = q_ref.shape
    scale = dh ** -0.5
    lane = lax.broadcasted_iota(jnp.int32, g.shape, 1)
    ig = jnp.sum(jnp.where(lane == h, g, 0.0), axis=1, keepdims=True)
    fpre = jnp.sum(jnp.where(lane == heads + h, g, 0.0), axis=1, keepdims=True)
    lf = jnp.minimum(fpre, 0.0) - jnp.log1p(jnp.exp(-jnp.abs(fpre)))

    row = lax.broadcasted_iota(jnp.int32, (L, L), 0)
    col = lax.broadcasted_iota(jnp.int32, (L, L), 1)
    causal = col <= row
    bcum = jnp.dot(causal.astype(F32), jnp.broadcast_to(lf, (L, LANES)),
                   precision=lax.Precision.HIGHEST, preferred_element_type=F32)[:, :1]
    r_row = jnp.transpose(jnp.broadcast_to(ig - bcum, (L, LANES)))[:1, :]

    m_prev = m_ref[:1, :1]
    dlog = jnp.where(causal, bcum + r_row, -jnp.inf)
    inter_log = bcum + m_prev
    m_t = jnp.maximum(inter_log, jnp.max(dlog, axis=1, keepdims=True))
    dw = jnp.exp(dlog - m_t)
    inter_w = jnp.exp(inter_log - m_t)

    qb = q_ref[...]
    kb = k_ref[...]
    scores = lax.dot_general(qb, kb, (((1,), (1,)), ((), ())), preferred_element_type=F32) * (dw * scale)
    scores_b = scores.astype(BF16)
    qn = jnp.sum(qb.astype(F32) * n_ref[:1, :], axis=1, keepdims=True)
    den = jnp.sum(scores, axis=1, keepdims=True) + inter_w * qn
    inv_den = 1.0 / jnp.maximum(jnp.abs(den), jnp.exp(-m_t))

    width = min(MXU_DIM, dh)
    col_blocks = [slice(j * width, (j + 1) * width) for j in range(dh // width)]
    row_sum = jnp.zeros((L, 1), F32)
    for cols in col_blocks:
        num = (jnp.dot(scores_b, v_ref[:, cols], preferred_element_type=F32)
               + inter_w * jnp.dot(qb, cb_ref[:, cols], preferred_element_type=F32))
        hblk = num * inv_den
        hh_ref[:, cols] = hblk
        row_sum = row_sum + jnp.sum(hblk, axis=1, keepdims=True)

    b_last = bcum[L - 1:L, :]
    wlog = b_last - bcum + ig
    m_new = jnp.maximum(b_last + m_prev, jnp.max(wlog, axis=0, keepdims=True))
    ws = jnp.exp(wlog - m_new)
    cw = jnp.exp(b_last + m_prev - m_new)
    kw = kb.astype(F32) * (ws * scale)
    n_ref[...] = jnp.broadcast_to(cw * n_ref[:1, :] + jnp.sum(kw, axis=0, keepdims=True), n_ref.shape)
    m_ref[...] = jnp.broadcast_to(m_new, m_ref.shape)
    kwt_ref[...] = jnp.transpose(kw).astype(BF16)
    for cols in col_blocks:
        c_new = cw * c_ref[:, cols] + jnp.dot(kwt_ref[...], v_ref[:, cols], preferred_element_type=F32)
        c_ref[:, cols] = c_new
        cb_ref[:, cols] = c_new.astype(BF16)

    mu = row_sum * (1.0 / dh)
    sq_sum = jnp.zeros((L, 1), F32)
    for cols in col_blocks:
        d = hh_ref[:, cols] - mu
        sq_sum = sq_sum + jnp.sum(d * d, axis=1, keepdims=True)
    rstd = lax.rsqrt(sq_sum * (1.0 / dh) + LN_EPS)
    for cols in col_blocks:
        hn = (hh_ref[:, cols] - mu) * rstd * hng_ref[:, cols]
        out = jax.nn.sigmoid(z_ref[:, cols].astype(F32)) * (hn + skip_ref[:, cols] * xc_ref[:, cols].astype(F32))
        o_ref[:, cols] = out.astype(o_ref.dtype)


def mlstm(q, k, v, xz, xc, gates, b_igate, b_fgate, head_norm_g, skip, *, chunk, heads_per_step):
    S, C = q.shape
    heads = b_igate.shape[0]
    dh = C // heads
    hp = heads_per_step
    gbias = jnp.zeros((1, LANES), F32).at[0, :heads].set(b_igate).at[0, heads:2 * heads].set(b_fgate)
    blk = pl.BlockSpec((chunk, hp * dh), lambda p, c: (c, p))
    vec = pl.BlockSpec((1, hp * dh), lambda p, c: (0, p))
    return pl.pallas_call(
        functools.partial(_mlstm_kernel, heads=heads), grid=(heads // hp, S // chunk),
        in_specs=[blk, blk, blk,
                  pl.BlockSpec((chunk, hp * dh), lambda p, c: (c, heads // hp + p)),
                  blk,
                  pl.BlockSpec((chunk, LANES), lambda p, c: (c, 0)),
                  pl.BlockSpec((1, LANES), lambda p, c: (0, 0)),
                  vec, vec],
        out_specs=blk,
        out_shape=jax.ShapeDtypeStruct((S, C), BF16),
        scratch_shapes=[pltpu.VMEM((hp, dh, dh), F32), pltpu.VMEM((hp, dh, dh), BF16),
                        pltpu.VMEM((hp, 8, dh), F32), pltpu.VMEM((hp, 8, LANES), F32),
                        pltpu.VMEM((hp, chunk, dh), F32), pltpu.VMEM((hp, dh, chunk), BF16)],
        compiler_params=_params(2), name="l0_mlstm")(
            q, k, v, xz, xc, gates, gbias, head_norm_g.reshape(1, C), skip.reshape(1, C))


def _spatial_gate_kernel(u_ref, v_ref, ng_ref, nb_ref, ws_ref, bs_ref, o_ref, *, groups):
    T = v_ref.shape[0]
    gd = v_ref.shape[1] // groups
    vn = _layer_norm_rows(v_ref[...].astype(F32), ng_ref[...], nb_ref[...]).astype(BF16)
    row = lax.broadcasted_iota(jnp.int32, (T, T), 0)
    col = lax.broadcasted_iota(jnp.int32, (T, T), 1)
    causal = col <= row
    bs = bs_ref[...]
    for g in range(groups):
        wc = jnp.where(causal, ws_ref[g], 0.0).astype(BF16)
        sv = jnp.dot(wc, vn[:, g * gd:(g + 1) * gd], preferred_element_type=F32) + bs[:, g:g + 1]
        o_ref[:, g * gd:(g + 1) * gd] = (u_ref[:, g * gd:(g + 1) * gd].astype(F32) * sv).astype(o_ref.dtype)


def spatial_gate(uv, norm_g, norm_b, w_s, b_s):
    S = uv.shape[0]
    W = norm_g.shape[0]
    groups, T, _ = w_s.shape
    bs_cols = jnp.zeros((T, LANES), F32).at[:, :groups].set(b_s.T)
    return pl.pallas_call(
        functools.partial(_spatial_gate_kernel, groups=groups), grid=(S // T,),
        in_specs=[pl.BlockSpec((T, W), lambda c: (c, 0)),
                  pl.BlockSpec((T, W), lambda c: (c, 1)),
                  pl.BlockSpec((1, W), lambda c: (0, 0)),
                  pl.BlockSpec((1, W), lambda c: (0, 0)),
                  pl.BlockSpec((groups, T, T), lambda c: (0, 0, 0)),
                  pl.BlockSpec((T, LANES), lambda c: (0, 0))],
        out_specs=pl.BlockSpec((T, W), lambda c: (c, 0)),
        out_shape=jax.ShapeDtypeStruct((S, W), BF16),
        compiler_params=_params(1), name="l1_spatial_gate")(
            uv, uv, norm_g.reshape(1, W), norm_b.reshape(1, W), w_s, bs_cols)


INFO_IDX0, INFO_IDX1, INFO_GATE0, INFO_GATE1, INFO_RANK0, INFO_RANK1 = range(6)


def _router_kernel(x_ref, w_ref, b_ref, info_ref, count_ref, carry_ref, *, n_experts):
    i = pl.program_id(0)
    tm = x_ref.shape[0]

    @pl.when(i == 0)
    def _():
        carry_ref[...] = jnp.zeros_like(carry_ref)

    x = x_ref[...]
    w = w_ref[...]
    x_hi = x.astype(BF16)
    x_lo = (x - x_hi.astype(F32)).astype(BF16)
    w_hi = w.astype(BF16)
    w_lo = (w - w_hi.astype(F32)).astype(BF16)
    logits = (jnp.dot(x_hi, w_hi, preferred_element_type=F32) + jnp.dot(x_lo, w_hi, preferred_element_type=F32)
              + jnp.dot(x_hi, w_lo, preferred_element_type=F32)) + b_ref[...]
    lane = lax.broadcasted_iota(jnp.int32, logits.shape, 1).astype(F32)
    neg_inf = -jnp.inf
    logits = jnp.where(lane < n_experts, logits, neg_inf)
    m0 = jnp.max(logits, axis=1, keepdims=True)
    i0 = jnp.min(jnp.where(logits == m0, lane, float(LANES)), axis=1, keepdims=True)
    rest = jnp.where(lane == i0, neg_inf, logits)
    m1 = jnp.max(rest, axis=1, keepdims=True)
    i1 = jnp.min(jnp.where(rest == m1, lane, float(LANES)), axis=1, keepdims=True)
    e1 = jnp.exp(m1 - m0)
    denom = 1.0 + e1
    g0 = 1.0 / denom
    g1 = e1 / denom
    hot0 = lane == i0
    hot1 = lane == i1
    member = jnp.where(hot0 | hot1, 1.0, 0.0)
    row = lax.broadcasted_iota(jnp.int32, (tm, tm), 0)
    col = lax.broadcasted_iota(jnp.int32, (tm, tm), 1)
    before = jnp.where(col < row, 1.0, 0.0).astype(BF16)
    rank = jnp.dot(before, member.astype(BF16), preferred_element_type=F32) + carry_ref[:1, :]
    r0 = jnp.sum(jnp.where(hot0, rank, 0.0), axis=1, keepdims=True)
    r1 = jnp.sum(jnp.where(hot1, rank, 0.0), axis=1, keepdims=True)
    info = jnp.zeros_like(logits)
    for slot, val in ((INFO_IDX0, i0), (INFO_IDX1, i1), (INFO_GATE0, g0), (INFO_GATE1, g1),
                      (INFO_RANK0, r0), (INFO_RANK1, r1)):
        info = jnp.where(lane == slot, val, info)
    info_ref[...] = info
    total = carry_ref[...] + jnp.sum(member, axis=0, keepdims=True)
    carry_ref[...] = total
    count_ref[...] = total


def router(x, router_w, router_b, *, tm):
    S, D = x.shape
    E = router_w.shape[1]
    w = jnp.zeros((D, LANES), F32).at[:, :E].set(router_w)
    b = jnp.zeros((1, LANES), F32).at[0, :E].set(router_b)
    return pl.pallas_call(
        functools.partial(_router_kernel, n_experts=E), grid=(S // tm,),
        in_specs=[pl.BlockSpec((tm, D), lambda i: (i, 0)),
                  pl.BlockSpec((D, LANES), lambda i: (0, 0)),
                  pl.BlockSpec((1, LANES), lambda i: (0, 0))],
        out_specs=[pl.BlockSpec((tm, LANES), lambda i: (i, 0)),
                   pl.BlockSpec((8, LANES), lambda i: (0, 0))],
        out_shape=[jax.ShapeDtypeStruct((S, LANES), F32), jax.ShapeDtypeStruct((8, LANES), F32)],
        scratch_shapes=[pltpu.VMEM((8, LANES), F32)],
        compiler_params=_params(1), name="l1_router")(x, w, b)


def _row_copy(src_hbm, dst_vmem, sem, src_row, dst_row):
    return pltpu.make_async_copy(src_hbm.at[pl.ds(src_row, 1), :], dst_vmem.at[pl.ds(dst_row, 1), :], sem)


def _gather_rows_kernel(row_src_ref, n_used_ref, x_hbm, o_ref, buf_ref, sem):
    t = pl.program_id(0)
    tm = o_ref.shape[0]
    n_used = n_used_ref[0]

    def start_tile(tile, slot):
        def start(pair, carry):
            for queue in range(2):
                r = 2 * pair + queue
                _row_copy(x_hbm, buf_ref.at[slot], sem.at[slot], row_src_ref[tile * tm + r], r).start(priority=queue)
            return carry
        lax.fori_loop(0, tm // 2, start, 0)

    @pl.when(t == 0)
    def _():
        start_tile(0, 0)

    @pl.when(t + 1 < n_used)
    def _():
        start_tile(t + 1, (t + 1) % 2)

    @pl.when(t < n_used)
    def _():
        slot = t % 2

        def wait(r, carry):
            _row_copy(x_hbm, buf_ref.at[slot], sem.at[slot], 0, r).wait()
            return carry
        lax.fori_loop(0, tm, wait, 0)
        o_ref[...] = buf_ref[slot]

    @pl.when(t >= n_used)
    def _():
        o_ref[...] = jnp.zeros_like(o_ref)


def gather_rows(x, row_src, n_used, *, tm):
    P = row_src.shape[0]
    D = x.shape[1]
    grid_spec = pltpu.PrefetchScalarGridSpec(
        num_scalar_prefetch=2, grid=(P // tm,),
        in_specs=[pl.BlockSpec(memory_space=pl.ANY)],
        out_specs=pl.BlockSpec((tm, D), lambda t, rs, nu: (t, 0)),
        scratch_shapes=[pltpu.VMEM((2, tm, D), x.dtype), pltpu.SemaphoreType.DMA((2,))])
    return pl.pallas_call(
        _gather_rows_kernel, grid_spec=grid_spec,
        out_shape=jax.ShapeDtypeStruct((P, D), x.dtype),
        compiler_params=_params(1), name="l1_moe_gather")(row_src, n_used, x)


def _expert_up_tile(xp_ref, w1_ref, w3_ref, o_ref):
    lo, hi = _unpack_bf16_pair(xp_ref[...])
    lo = lo.astype(BF16)
    hi = hi.astype(BF16)
    half = xp_ref.shape[1]

    def mm(w_ref):
        return (jnp.dot(lo, w_ref[:half, :].astype(BF16), preferred_element_type=F32)
                + jnp.dot(hi, w_ref[half:, :].astype(BF16), preferred_element_type=F32))

    o_ref[...] = (jax.nn.silu(mm(w1_ref)) * mm(w3_ref)).astype(o_ref.dtype)


def _expert_down_tile(h_ref, w_ref, o_ref):
    y = jnp.dot(h_ref[...], w_ref[...].astype(BF16), preferred_element_type=F32)
    half = y.shape[1] // 2
    o_ref[...] = _pack_bf16_pair(y[:, :half], y[:, half:])


def _grouped_kernel(te_ref, n_used_ref, *refs, tile_body):
    o_ref = refs[-1]
    used = pl.program_id(1) < n_used_ref[0]

    @pl.when(used)
    def _():
        tile_body(*refs)

    @pl.when(jnp.logical_not(used))
    def _():
        o_ref[...] = jnp.zeros_like(o_ref)


def _grouped_call(tile_body, x, weights, tile_expert, n_used, *, tm, tn, out_cols, out_dtype, name):
    P, xcols = x.shape
    _, K, N = weights[0].shape
    n_col_tiles = N // tn

    def row_map(j, t, te, nu):
        return (jnp.maximum(jnp.minimum(t, nu[0] - 1), 0), 0)

    def out_map(j, t, te, nu):
        return (t, j)

    def w_map(j, t, te, nu):
        return (te[t], 0, j)

    grid_spec = pltpu.PrefetchScalarGridSpec(
        num_scalar_prefetch=2, grid=(n_col_tiles, P // tm),
        in_specs=[pl.BlockSpec((tm, xcols), row_map)] + [pl.BlockSpec((None, K, tn), w_map)] * len(weights),
        out_specs=pl.BlockSpec((tm, out_cols // n_col_tiles), out_map))
    return pl.pallas_call(
        functools.partial(_grouped_kernel, tile_body=tile_body), grid_spec=grid_spec,
        out_shape=jax.ShapeDtypeStruct((P, out_cols), out_dtype),
        compiler_params=_params(2), name=name)(tile_expert, n_used, x, *weights)


def _combine_ln_kernel(pos0_ref, pos1_ref, y_hbm, x_ref, info_ref, g_ref, b_ref, o_ref,
                       buf0_ref, buf1_ref, sem, *, col_tiles):
    i = pl.program_id(0)
    tm = x_ref.shape[0]

    def start(r, carry):
        _row_copy(y_hbm, buf0_ref, sem, pos0_ref[i * tm + r], r).start(priority=0)
        _row_copy(y_hbm, buf1_ref, sem, pos1_ref[i * tm + r], r).start(priority=1)
        return carry

    def wait(r, carry):
        _row_copy(y_hbm, buf0_ref, sem, 0, r).wait()
        _row_copy(y_hbm, buf1_ref, sem, 0, r).wait()
        return carry

    lax.fori_loop(0, tm, start, 0)
    lax.fori_loop(0, tm, wait, 0)
    info = info_ref[...]
    g0 = info[:, INFO_GATE0:INFO_GATE0 + 1]
    g1 = info[:, INFO_GATE1:INFO_GATE1 + 1]
    lo0, hi0 = _unpack_bf16_pair(buf0_ref[...])
    lo1, hi1 = _unpack_bf16_pair(buf1_ref[...])
    lo = g0 * lo0 + g1 * lo1
    hi = g0 * hi0 + g1 * hi1
    w = lo.shape[1] // col_tiles
    y = jnp.concatenate([part[:, j * w:(j + 1) * w] for j in range(col_tiles) for part in (lo, hi)], axis=1)
    o_ref[...] = _layer_norm_rows(ALPHA * x_ref[...] + y, g_ref[...], b_ref[...])


def combine_ln(y_rows, pos0, pos1, info, x, g, b, *, tm, col_tiles):
    S, D = x.shape
    half = y_rows.shape[1]
    grid_spec = pltpu.PrefetchScalarGridSpec(
        num_scalar_prefetch=2, grid=(S // tm,),
        in_specs=[pl.BlockSpec(memory_space=pl.ANY),
                  pl.BlockSpec((tm, D), lambda i, p0, p1: (i, 0)),
                  pl.BlockSpec((tm, LANES), lambda i, p0, p1: (i, 0)),
                  pl.BlockSpec((1, D), lambda i, p0, p1: (0, 0)),
                  pl.BlockSpec((1, D), lambda i, p0, p1: (0, 0))],
        out_specs=pl.BlockSpec((tm, D), lambda i, p0, p1: (i, 0)),
        scratch_shapes=[pltpu.VMEM((tm, half), jnp.uint32), pltpu.VMEM((tm, half), jnp.uint32),
                        pltpu.SemaphoreType.DMA(())])
    return pl.pallas_call(
        functools.partial(_combine_ln_kernel, col_tiles=col_tiles), grid_spec=grid_spec,
        out_shape=jax.ShapeDtypeStruct((S, D), F32),
        compiler_params=_params(1), name="l1_moe_combine_ln")(
            pos0, pos1, y_rows, x, info, g.reshape(1, D), b.reshape(1, D))


def moe_layer(x, xp, router_w, router_b, w1, w3, w2, ln_g, ln_b, *, tm_route, tm, tn_up, tn_down, tm_combine):
    S, D = x.shape
    E = router_w.shape[1]
    info, counts = router(x, router_w, router_b, tm=tm_route)
    idx0 = info[:, INFO_IDX0].astype(jnp.int32)
    idx1 = info[:, INFO_IDX1].astype(jnp.int32)
    count = counts[0, :E].astype(jnp.int32)
    tiles = (count + tm - 1) // tm
    tile_end = jnp.cumsum(tiles)
    offset = (tile_end - tiles) * tm
    n_used = tile_end[-1:]
    pos0 = offset[idx0] + info[:, INFO_RANK0].astype(jnp.int32)
    pos1 = offset[idx1] + info[:, INFO_RANK1].astype(jnp.int32)
    n_rows = TOP_K * S + E * tm
    n_tiles = n_rows // tm
    tile_id = jnp.minimum(jnp.arange(n_tiles, dtype=jnp.int32), n_used[0] - 1)
    tile_expert = jnp.sum(tile_id[:, None] >= tile_end[None, :], axis=1).astype(jnp.int32)
    token = jnp.arange(S, dtype=jnp.int32)
    row_src = jnp.zeros((n_rows,), jnp.int32).at[pos0].set(token).at[pos1].set(token)

    xs = gather_rows(xp, row_src, n_used, tm=tm)
    hs = _grouped_call(_expert_up_tile, xs, (w1, w3), tile_expert, n_used,
                       tm=tm, tn=tn_up, out_cols=w1.shape[2], out_dtype=BF16, name="l1_moe_up")
    ys = _grouped_call(_expert_down_tile, hs, (w2,), tile_expert, n_used,
                       tm=tm, tn=tn_down, out_cols=D // 2, out_dtype=jnp.uint32, name="l1_moe_down")
    return combine_ln(ys, pos0, pos1, info, x, ln_g, ln_b, tm=tm_combine, col_tiles=D // tn_down)


def kernel(x, l0_mix_w_in, l0_conv_w, l0_conv_b, l0_w_q, l0_w_k, l0_w_v, l0_w_gates, l0_b_igate, l0_b_fgate, l0_head_norm_g, l0_skip, l0_mix_w_out, l0_ln1_g, l0_ln1_b, l0_ffn_w1, l0_ffn_w3, l0_ffn_w2, l0_ln2_g, l0_ln2_b, l1_mix_w_in, l1_mix_b_in, l1_sg_norm_g, l1_sg_norm_b, l1_sg_w, l1_sg_b, l1_mix_w_out, l1_ln1_g, l1_ln1_b, l1_router_w, l1_router_b, l1_exp_w1, l1_exp_w3, l1_exp_w2, l1_ln2_g, l1_ln2_b):
    B, S, D = x.shape
    x0 = x.reshape(B * S, D)

    def bf(w):
        return w.astype(BF16)

    tm_mm = min(MM_ROW_TILE, S)
    tm_ln = min(LN_ROW_TILE, S)
    xz = matmul(bf(x0), l0_mix_w_in, tm=tm_mm, tn=MM_COL_TILE, name="l0_in_proj")
    xc, q, k, v, gates = conv_qkv(xz, l0_conv_w, l0_conv_b, l0_w_q, l0_w_k, l0_w_v, l0_w_gates,
                                  tm=min(CONV_ROW_TILE, S), tc=min(CONV_CHANNEL_TILE, l0_conv_w.shape[1]))
    hg = mlstm(q, k, v, xz, xc, gates, l0_b_igate, l0_b_fgate, l0_head_norm_g, l0_skip,
               chunk=MLSTM_KERNEL_CHUNK, heads_per_step=MLSTM_HEADS_PER_STEP)
    x1, x1b = matmul_residual_ln(hg, bf(l0_mix_w_out), x0, l0_ln1_g, l0_ln1_b,
                                 tm=tm_ln, max_tk=LN_K_TILE,packed=False, name="l0_out_proj_ln")
    h = matmul_swiglu(x1b, l0_ffn_w1, l0_ffn_w3, tm=tm_mm, tn=SWIGLU_COL_TILE, name="l0_ffn_up")
    x2, x2b = matmul_residual_ln(h, bf(l0_ffn_w2), x1, l0_ln2_g, l0_ln2_b,
                                 tm=tm_ln, max_tk=LN_K_TILE,packed=False, name="l0_ffn_down_ln")
    uv = matmul(x2b, l1_mix_w_in, l1_mix_b_in, tm=tm_mm, tn=MM_COL_TILE, name="l1_in_proj_gelu")
    gated = spatial_gate(uv, l1_sg_norm_g, l1_sg_norm_b, l1_sg_w, l1_sg_b)
    x3, x3p = matmul_residual_ln(gated, bf(l1_mix_w_out), x2, l1_ln1_g, l1_ln1_b,
                                 tm=tm_ln, max_tk=LN_K_TILE,packed=True, name="l1_out_proj_ln")
    y = moe_layer(x3, x3p, l1_router_w, l1_router_b, l1_exp_w1, l1_exp_w3, l1_exp_w2, l1_ln2_g, l1_ln2_b,
                  tm_route=min(ROUTER_ROW_TILE, S), tm=EXPERT_ROW_TILE,
                  tn_up=min(EXPERT_UP_COL_TILE, l1_exp_w1.shape[2]), tn_down=min(EXPERT_DOWN_COL_TILE, D),
                  tm_combine=min(COMBINE_ROW_TILE, S))
    return y.reshape(B, S, D)
```

```python
import functools

import jax
import jax.numpy as jnp
from jax import lax
from jax.experimental import pallas as pl
from jax.experimental.pallas import tpu as pltpu

F32 = jnp.float32
BF16 = jnp.bfloat16

MLSTM_HEADS = 8
MLSTM_QKV_BLOCK = 4
MLSTM_CONV_WIDTH = 4
SG_CHUNK = 128
SG_GROUPS = 8
N_EXPERTS = 8
TOP_K = 2
DEPTH = 2
ALPHA = (2 * DEPTH) ** 0.25
LN_EPS = 1e-5

LANES = 128
BF16_SUBLANES = 16
MXU_DIM = 256
VMEM_LIMIT_BYTES = 56 * 1024 * 1024

MLSTM_KERNEL_CHUNK = 256
MLSTM_HEADS_PER_STEP = 2

MM_ROW_TILE = 1024
MM_COL_TILE = 1024
SWIGLU_COL_TILE = 512
LN_ROW_TILE = 512
LN_K_TILE = 1792
CONV_ROW_TILE = 512
CONV_CHANNEL_TILE = 1024
ROUTER_ROW_TILE = 512
EXPERT_ROW_TILE = 512
EXPERT_UP_COL_TILE = 512
EXPERT_DOWN_COL_TILE = 1024
COMBINE_ROW_TILE = 256


def _params(n_axes):
    return pltpu.CompilerParams(dimension_semantics=("arbitrary",) * n_axes,
                                vmem_limit_bytes=VMEM_LIMIT_BYTES)


def _layer_norm_rows(y, g, b):
    mu = jnp.mean(y, axis=-1, keepdims=True)
    d = y - mu
    var = jnp.mean(d * d, axis=-1, keepdims=True)
    return d * lax.rsqrt(var + LN_EPS) * g + b


def _pack_bf16_pair(lo, hi):
    lo_bits = lax.bitcast_convert_type(lo.astype(BF16).astype(F32), jnp.uint32) >> 16
    hi_bits = lax.bitcast_convert_type(hi.astype(BF16).astype(F32), jnp.uint32) & jnp.uint32(0xFFFF0000)
    return hi_bits | lo_bits


def _unpack_bf16_pair(p):
    lo = lax.bitcast_convert_type(p << 16, F32)
    hi = lax.bitcast_convert_type(p & jnp.uint32(0xFFFF0000), F32)
    return lo, hi


def _call_with_rounding_job(body, grid, in_specs, out_specs, out_shape, args, to_round, **call_kwargs):
    if to_round is None:
        return pl.pallas_call(body, grid=grid, in_specs=in_specs, out_specs=out_specs, out_shape=out_shape,
                              **call_kwargs)(*args)
    rows, cols = to_round.shape
    steps = grid[0] * grid[1]
    slab = rows // steps
    assert rows % steps == 0 and slab % BF16_SUBLANES == 0, (to_round.shape, grid)
    slab_spec = pl.BlockSpec((slab, cols), lambda a, b: (a * grid[1] + b, 0))
    n_in, n_out = len(in_specs), len(out_specs)

    def body_and_round(*refs):
        src_ref = refs[n_in]
        dst_ref = refs[n_in + 1 + n_out]
        dst_ref[...] = src_ref[...].astype(BF16)
        body(*refs[:n_in], *refs[n_in + 1:n_in + 1 + n_out], *refs[n_in + 2 + n_out:])

    return pl.pallas_call(
        body_and_round, grid=grid, in_specs=list(in_specs) + [slab_spec],
        out_specs=list(out_specs) + [slab_spec],
        out_shape=list(out_shape) + [jax.ShapeDtypeStruct((rows, cols), BF16)],
        **call_kwargs)(*args, to_round)


def _mm_kernel(x_ref, w_ref, o_ref):
    o_ref[...] = jnp.dot(x_ref[...], w_ref[...].astype(BF16), preferred_element_type=F32).astype(o_ref.dtype)


def _mm_bias_gelu_kernel(x_ref, w_ref, b_ref, o_ref):
    y = jnp.dot(x_ref[...], w_ref[...].astype(BF16), preferred_element_type=F32) + b_ref[...]
    o_ref[...] = jax.nn.gelu(y).astype(o_ref.dtype)


def matmul(x, w, bias=None, *, tm, tn, name, to_round=None):
    M, K = x.shape
    N = w.shape[1]
    in_specs = [pl.BlockSpec((tm, K), lambda i, j: (i, 0), pipeline_mode=pl.Buffered(1)),
                pl.BlockSpec((K, tn), lambda i, j: (0, j))]
    args = [x, w]
    body = _mm_kernel
    if bias is not None:
        in_specs.append(pl.BlockSpec((1, tn), lambda i, j: (0, j)))
        args.append(bias.reshape(1, N).astype(F32))
        body = _mm_bias_gelu_kernel
    return _call_with_rounding_job(
        body, (M // tm, N // tn), in_specs, [pl.BlockSpec((tm, tn), lambda i, j: (i, j))],
        [jax.ShapeDtypeStruct((M, N), BF16)], args, to_round, compiler_params=_params(2), name=name)


def _mm_swiglu_kernel(x_ref, w1_ref, w3_ref, o_ref):
    x = x_ref[...]
    a = jnp.dot(x, w1_ref[...].astype(BF16), preferred_element_type=F32)
    b = jnp.dot(x, w3_ref[...].astype(BF16), preferred_element_type=F32)
    o_ref[...] = (jax.nn.silu(a) * b).astype(o_ref.dtype)


def matmul_swiglu(x, w1, w3, *, tm, tn, name, to_round=None):
    M, K = x.shape
    N = w1.shape[1]
    wspec = pl.BlockSpec((K, tn), lambda i, j: (0, j))
    return _call_with_rounding_job(
        _mm_swiglu_kernel, (M // tm, N // tn),
        [pl.BlockSpec((tm, K), lambda i, j: (i, 0), pipeline_mode=pl.Buffered(1)), wspec, wspec],
        [pl.BlockSpec((tm, tn), lambda i, j: (i, j))], [jax.ShapeDtypeStruct((M, N), BF16)],
        [x, w1, w3], to_round, compiler_params=_params(2), name=name)


def _mm_res_ln_kernel(a_ref, w_ref, r_ref, g_ref, b_ref, o32_ref, o2_ref, acc0_ref, acc1_ref, *, n_row_blocks):
    i = pl.program_id(0)
    k = pl.program_id(1)
    rc, n = o32_ref.shape

    @pl.when(jnp.logical_and(i == 0, k == 0))
    def _():
        acc0_ref[...] = jnp.zeros_like(acc0_ref)
        acc1_ref[...] = jnp.zeros_like(acc1_ref)

    def finish_slice(done_ref):
        rows = pl.ds(pl.multiple_of(k * rc, rc), rc)
        y = ALPHA * r_ref[...] + done_ref[rows, :]
        done_ref[rows, :] = jnp.zeros((rc, n), F32)
        out = _layer_norm_rows(y, g_ref[...], b_ref[...])
        o32_ref[...] = out
        if o2_ref.dtype == jnp.uint32:
            o2_ref[...] = _pack_bf16_pair(out[:, :n // 2], out[:, n // 2:])
        else:
            o2_ref[...] = out.astype(o2_ref.dtype)

    def accumulate(acc_ref):
        acc_ref[...] += jnp.dot(a_ref[...], w_ref[...], preferred_element_type=F32)

    accs = (acc0_ref, acc1_ref)
    interior = jnp.logical_and(i > 0, i < n_row_blocks)

    @pl.when(i == 0)
    def _():
        accumulate(acc0_ref)

    for parity in range(2):
        @pl.when(jnp.logical_and(interior, i % 2 == parity))
        def _():
            finish_slice(accs[1 - parity])
            accumulate(accs[parity])

    @pl.when(i == n_row_blocks)
    def _():
        finish_slice(accs[(n_row_blocks - 1) % 2])


def _ln_k_steps(K, tm, max_tk):
    for nk in (1, 2, 4, 8, 16, 32, 64):
        if K % nk == 0 and (K // nk) % MXU_DIM == 0 and K // nk <= max_tk and tm % (8 * nk) == 0:
            return nk
    raise ValueError(f"no K tiling for K={K}, tm={tm}")


def matmul_residual_ln(a, w, resid, g, b, *, tm, max_tk, packed, name):
    M, K = a.shape
    N = w.shape[1]
    nk = _ln_k_steps(K, tm, max_tk)
    tk = K // nk
    rc = tm // nk
    n_row_blocks = M // tm
    last = n_row_blocks - 1

    def a_map(i, k):
        return (jnp.minimum(i, last), jnp.where(i <= last, k, nk - 1))

    def w_map(i, k):
        return (jnp.where(i <= last, k, nk - 1), 0)

    def slice_map(i, k):
        return (jnp.where(i == 0, 0, (i - 1) * nk + k), 0)

    vec_spec = pl.BlockSpec((1, N), lambda i, k: (0, 0))
    out_shape = [jax.ShapeDtypeStruct((M, N), F32)]
    out_specs = [pl.BlockSpec((rc, N), slice_map)]
    if packed:
        out_shape.append(jax.ShapeDtypeStruct((M, N // 2), jnp.uint32))
        out_specs.append(pl.BlockSpec((rc, N // 2), slice_map))
    else:
        out_shape.append(jax.ShapeDtypeStruct((M, N), BF16))
        out_specs.append(pl.BlockSpec((rc, N), slice_map))
    return pl.pallas_call(
        functools.partial(_mm_res_ln_kernel, n_row_blocks=n_row_blocks), grid=(n_row_blocks + 1, nk),
        in_specs=[pl.BlockSpec((tm, tk), a_map), pl.BlockSpec((tk, N), w_map),
                  pl.BlockSpec((rc, N), slice_map), vec_spec, vec_spec],
        out_specs=out_specs, out_shape=out_shape,
        scratch_shapes=[pltpu.VMEM((tm, N), F32), pltpu.VMEM((tm, N), F32)],
        compiler_params=_params(2), name=name)(a, w, resid, g.reshape(1, N), b.reshape(1, N))


def _shift_rows(cur, halo, s):
    rolled = pltpu.roll(cur, s, 0)
    halo_top = pltpu.roll(halo, s, 0)[:8]
    row = lax.broadcasted_iota(jnp.int32, (8, cur.shape[1]), 0)
    top = jnp.where(row < s, halo_top, rolled[:8])
    return jnp.concatenate([top, rolled[8:]], axis=0)


def _conv_qkv_kernel(cur_ref, halo_ref, cw_ref, cb_ref, wq_ref, wk_ref, wv_ref, wg_ref,
                     xc_ref, q_ref, k_ref, v_ref, gates_ref):
    i = pl.program_id(0)
    c = pl.program_id(1)
    cur_b = cur_ref[...]
    cur = cur_b.astype(F32)
    halo = jnp.where(i > 0, halo_ref[...].astype(F32), 0.0)
    cw = cw_ref[...]
    acc = cw[MLSTM_CONV_WIDTH - 1:MLSTM_CONV_WIDTH, :] * cur + cb_ref[...]
    for s in range(1, MLSTM_CONV_WIDTH):
        j = MLSTM_CONV_WIDTH - 1 - s
        acc = acc + cw[j:j + 1, :] * _shift_rows(cur, halo, s)
    xc_b = jax.nn.silu(acc).astype(BF16)
    xc_ref[...] = xc_b

    def block_diag(x_b, w_ref):
        n = x_b.shape[1] // MXU_DIM
        parts = [jnp.dot(x_b[:, MXU_DIM * j:MXU_DIM * (j + 1)], w_ref[j], preferred_element_type=F32)
                 for j in range(n)]
        return jnp.concatenate(parts, axis=1).astype(BF16)

    q_b = block_diag(xc_b, wq_ref)
    k_b = block_diag(xc_b, wk_ref)
    v_b = block_diag(cur_b, wv_ref)
    q_ref[...] = q_b
    k_ref[...] = k_b
    v_ref[...] = v_b
    part = (jnp.dot(q_b, wg_ref[0], preferred_element_type=F32)
            + jnp.dot(k_b, wg_ref[1], preferred_element_type=F32)
            + jnp.dot(v_b, wg_ref[2], preferred_element_type=F32))

    @pl.when(c == 0)
    def _():
        gates_ref[...] = jnp.zeros_like(gates_ref)

    gates_ref[...] += part


def _expand_block_diag(w):
    nb, blk, _ = w.shape
    per = MXU_DIM // blk
    wt = w.reshape(nb // per, per, blk, blk)
    eye = jnp.eye(per, dtype=w.dtype)
    full = jnp.einsum("tpcd,pq->tpcqd", wt, eye)
    return full.reshape(nb // per, MXU_DIM, MXU_DIM).astype(BF16)


def conv_qkv(xz, conv_w, conv_b, w_q, w_k, w_v, w_gates, *, tm, tc):
    S = xz.shape[0]
    C = conv_w.shape[1]
    ng = w_gates.shape[2]
    wg = jnp.zeros((3, C, LANES), BF16).at[:, :, :ng].set(w_gates.astype(BF16))
    tiles = tc // MXU_DIM
    halo_rows = BF16_SUBLANES
    blk = pl.BlockSpec((tm, tc), lambda i, c: (i, c))
    bd_spec = pl.BlockSpec((tiles, MXU_DIM, MXU_DIM), lambda i, c: (c, 0, 0))
    act = jax.ShapeDtypeStruct((S, C), BF16)
    return pl.pallas_call(
        _conv_qkv_kernel, grid=(S // tm, C // tc),
        in_specs=[blk,
                  pl.BlockSpec((halo_rows, tc), lambda i, c: (jnp.maximum(i * (tm // halo_rows) - 1, 0), c)),
                  pl.BlockSpec((MLSTM_CONV_WIDTH, tc), lambda i, c: (0, c)),
                  pl.BlockSpec((1, tc), lambda i, c: (0, c)),
                  bd_spec, bd_spec, bd_spec,
                  pl.BlockSpec((3, tc, LANES), lambda i, c: (0, c, 0))],
        out_specs=[blk, blk, blk, blk, pl.BlockSpec((tm, LANES), lambda i, c: (i, 0))],
        out_shape=[act, act, act, act, jax.ShapeDtypeStruct((S, LANES), F32)],
        compiler_params=_params(2), name="l0_conv_qkv")(
            xz, xz, conv_w, conv_b.reshape(1, C),
            _expand_block_diag(w_q), _expand_block_diag(w_k), _expand_block_diag(w_v), wg)


def _mlstm_kernel(q_ref, k_ref, v_ref, z_ref, xc_ref, gates_ref, gbias_ref, hng_ref, skip_ref,
                  o_ref, c_ref, cb_ref, n_ref, m_ref, hh_ref, kwt_ref, *, heads):
    heads_per_step = c_ref.shape[0]
    dh = q_ref.shape[1] // heads_per_step

    @pl.when(pl.program_id(1) == 0)
    def _():
        c_ref[...] = jnp.zeros_like(c_ref)
        cb_ref[...] = jnp.zeros_like(cb_ref)
        n_ref[...] = jnp.zeros_like(n_ref)
        m_ref[...] = jnp.zeros_like(m_ref)

    g = gates_ref[...] + gbias_ref[...]
    for i in range(heads_per_step):
        cols = slice(i * dh, (i + 1) * dh)
        _mlstm_head(pl.program_id(0) * heads_per_step + i, heads, g,
                    q_ref.at[:, cols], k_ref.at[:, cols], v_ref.at[:, cols], z_ref.at[:, cols], xc_ref.at[:, cols],
                    hng_ref.at[:, cols], skip_ref.at[:, cols], o_ref.at[:, cols],
                    c_ref.at[i], cb_ref.at[i], n_ref.at[i], m_ref.at[i], hh_ref.at[i], kwt_ref.at[i])


def _mlstm_head(h, heads, g, q_ref, k_ref, v_ref, z_ref, xc_ref, hng_ref, skip_ref,
                o_ref, c_ref, cb_ref, n_ref, m_ref, hh_ref, kwt_ref):
    L, dh = q_ref.shape
    scale = dh ** -0.5
    lane = lax.broadcasted_iota(jnp.int32, g.shape, 1)
    ig = jnp.sum(jnp.where(lane == h, g, 0.0), axis=1, keepdims=True)
    fpre = jnp.sum(jnp.where(lane == heads + h, g, 0.0), axis=1, keepdims=True)
    lf = jnp.minimum(fpre, 0.0) - jnp.log1p(jnp.exp(-jnp.abs(fpre)))

    row = lax.broadcasted_iota(jnp.int32, (L, L), 0)
    col = lax.broadcasted_iota(jnp.int32, (L, L), 1)
    causal = col <= row
    bcum = jnp.dot(causal.astype(F32), jnp.broadcast_to(lf, (L, LANES)),
                   precision=lax.Precision.HIGHEST, preferred_element_type=F32)[:, :1]
    r_row = jnp.transpose(jnp.broadcast_to(ig - bcum, (L, LANES)))[:1, :]

    m_prev = m_ref[:1, :1]
    dlog = jnp.where(causal, bcum + r_row, -jnp.inf)
    inter_log = bcum + m_prev
    m_t = jnp.maximum(inter_log, jnp.max(dlog, axis=1, keepdims=True))
    dw = jnp.exp(dlog - m_t)
    inter_w = jnp.exp(inter_log - m_t)

    qb = q_ref[...]
    kb = k_ref[...]
    scores = lax.dot_general(qb, kb, (((1,), (1,)), ((), ())), preferred_element_type=F32) * (dw * scale)
    scores_b = scores.astype(BF16)
    qn = jnp.sum(qb.astype(F32) * n_ref[:1, :], axis=1, keepdims=True)
    den = jnp.sum(scores, axis=1, keepdims=True) + inter_w * qn
    inv_den = 1.0 / jnp.maximum(jnp.abs(den), jnp.exp(-m_t))

    width = min(MXU_DIM, dh)
    col_blocks = [slice(j * width, (j + 1) * width) for j in range(dh // width)]
    row_sum = jnp.zeros((L, 1), F32)
    for cols in col_blocks:
        num = (jnp.dot(scores_b, v_ref[:, cols], preferred_element_type=F32)
               + inter_w * jnp.dot(qb, cb_ref[:, cols], preferred_element_type=F32))
        hblk = num * inv_den
        hh_ref[:, cols] = hblk
        row_sum = row_sum + jnp.sum(hblk, axis=1, keepdims=True)

    b_last = bcum[L - 1:L, :]
    wlog = b_last - bcum + ig
    m_new = jnp.maximum(b_last + m_prev, jnp.max(wlog, axis=0, keepdims=True))
    ws = jnp.exp(wlog - m_new)
    cw = jnp.exp(b_last + m_prev - m_new)
    kw = kb.astype(F32) * (ws * scale)
    n_ref[...] = jnp.broadcast_to(cw * n_ref[:1, :] + jnp.sum(kw, axis=0, keepdims=True), n_ref.shape)
    m_ref[...] = jnp.broadcast_to(m_new, m_ref.shape)
    kwt_ref[...] = jnp.transpose(kw).astype(BF16)
    for cols in col_blocks:
        c_new = cw * c_ref[:, cols] + jnp.dot(kwt_ref[...], v_ref[:, cols], preferred_element_type=F32)
        c_ref[:, cols] = c_new
        cb_ref[:, cols] = c_new.astype(BF16)

    mu = row_sum * (1.0 / dh)
    sq_sum = jnp.zeros((L, 1), F32)
    for cols in col_blocks:
        d = hh_ref[:, cols] - mu
        sq_sum = sq_sum + jnp.sum(d * d, axis=1, keepdims=True)
    rstd = lax.rsqrt(sq_sum * (1.0 / dh) + LN_EPS)
    for cols in col_blocks:
        hn = (hh_ref[:, cols] - mu) * rstd * hng_ref[:, cols]
        out = jax.nn.sigmoid(z_ref[:, cols].astype(F32)) * (hn + skip_ref[:, cols] * xc_ref[:, cols].astype(F32))
        o_ref[:, cols] = out.astype(o_ref.dtype)


def mlstm(q, k, v, xz, xc, gates, b_igate, b_fgate, head_norm_g, skip, *, chunk, heads_per_step, to_round=None):
    S, C = q.shape
    heads = b_igate.shape[0]
    dh = C // heads
    hp = heads_per_step
    gbias = jnp.zeros((1, LANES), F32).at[0, :heads].set(b_igate).at[0, heads:2 * heads].set(b_fgate)
    blk = pl.BlockSpec((chunk, hp * dh), lambda p, c: (c, p))
    vec = pl.BlockSpec((1, hp * dh), lambda p, c: (0, p))
    return _call_with_rounding_job(
        functools.partial(_mlstm_kernel, heads=heads), (heads // hp, S // chunk),
        [blk, blk, blk,
         pl.BlockSpec((chunk, hp * dh), lambda p, c: (c, heads // hp + p)),
         blk,
         pl.BlockSpec((chunk, LANES), lambda p, c: (c, 0)),
         pl.BlockSpec((1, LANES), lambda p, c: (0, 0)),
         vec, vec],
        [blk], [jax.ShapeDtypeStruct((S, C), BF16)],
        [q, k, v, xz, xc, gates, gbias, head_norm_g.reshape(1, C), skip.reshape(1, C)], to_round,
        scratch_shapes=[pltpu.VMEM((hp, dh, dh), F32), pltpu.VMEM((hp, dh, dh), BF16),
                        pltpu.VMEM((hp, 8, dh), F32), pltpu.VMEM((hp, 8, LANES), F32),
                        pltpu.VMEM((hp, chunk, dh), F32), pltpu.VMEM((hp, dh, chunk), BF16)],
        compiler_params=_params(2), name="l0_mlstm")


def _spatial_gate_kernel(u_ref, v_ref, ng_ref, nb_ref, ws_ref, bs_ref, o_ref, *, groups):
    T = v_ref.shape[0]
    gd = v_ref.shape[1] // groups
    vn = _layer_norm_rows(v_ref[...].astype(F32), ng_ref[...], nb_ref[...]).astype(BF16)
    row = lax.broadcasted_iota(jnp.int32, (T, T), 0)
    col = lax.broadcasted_iota(jnp.int32, (T, T), 1)
    causal = col <= row
    bs = bs_ref[...]
    for g in range(groups):
        wc = jnp.where(causal, ws_ref[g], 0.0).astype(BF16)
        sv = jnp.dot(wc, vn[:, g * gd:(g + 1) * gd], preferred_element_type=F32) + bs[:, g:g + 1]
        o_ref[:, g * gd:(g + 1) * gd] = (u_ref[:, g * gd:(g + 1) * gd].astype(F32) * sv).astype(o_ref.dtype)


def spatial_gate(uv, norm_g, norm_b, w_s, b_s):
    S = uv.shape[0]
    W = norm_g.shape[0]
    groups, T, _ = w_s.shape
    bs_cols = jnp.zeros((T, LANES), F32).at[:, :groups].set(b_s.T)
    return pl.pallas_call(
        functools.partial(_spatial_gate_kernel, groups=groups), grid=(S // T,),
        in_specs=[pl.BlockSpec((T, W), lambda c: (c, 0)),
                  pl.BlockSpec((T, W), lambda c: (c, 1)),
                  pl.BlockSpec((1, W), lambda c: (0, 0)),
                  pl.BlockSpec((1, W), lambda c: (0, 0)),
                  pl.BlockSpec((groups, T, T), lambda c: (0, 0, 0)),
                  pl.BlockSpec((T, LANES), lambda c: (0, 0))],
        out_specs=pl.BlockSpec((T, W), lambda c: (c, 0)),
        out_shape=jax.ShapeDtypeStruct((S, W), BF16),
        compiler_params=_params(1), name="l1_spatial_gate")(
            uv, uv, norm_g.reshape(1, W), norm_b.reshape(1, W), w_s, bs_cols)


INFO_IDX0, INFO_IDX1, INFO_GATE0, INFO_GATE1, INFO_RANK0, INFO_RANK1 = range(6)


def _router_kernel(x_ref, w_ref, b_ref, info_ref, count_ref, carry_ref, *, n_experts):
    i = pl.program_id(0)
    tm = x_ref.shape[0]

    @pl.when(i == 0)
    def _():
        carry_ref[...] = jnp.zeros_like(carry_ref)

    x = x_ref[...]
    w = w_ref[...]
    x_hi = x.astype(BF16)
    x_lo = (x - x_hi.astype(F32)).astype(BF16)
    w_hi = w.astype(BF16)
    w_lo = (w - w_hi.astype(F32)).astype(BF16)
    logits = (jnp.dot(x_hi, w_hi, preferred_element_type=F32) + jnp.dot(x_lo, w_hi, preferred_element_type=F32)
              + jnp.dot(x_hi, w_lo, preferred_element_type=F32)) + b_ref[...]
    lane = lax.broadcasted_iota(jnp.int32, logits.shape, 1).astype(F32)
    neg_inf = -jnp.inf
    logits = jnp.where(lane < n_experts, logits, neg_inf)
    m0 = jnp.max(logits, axis=1, keepdims=True)
    i0 = jnp.min(jnp.where(logits == m0, lane, float(LANES)), axis=1, keepdims=True)
    rest = jnp.where(lane == i0, neg_inf, logits)
    m1 = jnp.max(rest, axis=1, keepdims=True)
    i1 = jnp.min(jnp.where(rest == m1, lane, float(LANES)), axis=1, keepdims=True)
    e1 = jnp.exp(m1 - m0)
    denom = 1.0 + e1
    g0 = 1.0 / denom
    g1 = e1 / denom
    hot0 = lane == i0
    hot1 = lane == i1
    member = jnp.where(hot0 | hot1, 1.0, 0.0)
    row = lax.broadcasted_iota(jnp.int32, (tm, tm), 0)
    col = lax.broadcasted_iota(jnp.int32, (tm, tm), 1)
    before = jnp.where(col < row, 1.0, 0.0).astype(BF16)
    rank = jnp.dot(before, member.astype(BF16), preferred_element_type=F32) + carry_ref[:1, :]
    r0 = jnp.sum(jnp.where(hot0, rank, 0.0), axis=1, keepdims=True)
    r1 = jnp.sum(jnp.where(hot1, rank, 0.0), axis=1, keepdims=True)
    info = jnp.zeros_like(logits)
    for slot, val in ((INFO_IDX0, i0), (INFO_IDX1, i1), (INFO_GATE0, g0), (INFO_GATE1, g1),
                      (INFO_RANK0, r0), (INFO_RANK1, r1)):
        info = jnp.where(lane == slot, val, info)
    info_ref[...] = info
    total = carry_ref[...] + jnp.sum(member, axis=0, keepdims=True)
    carry_ref[...] = total
    count_ref[...] = total


def router(x, router_w, router_b, *, tm):
    S, D = x.shape
    E = router_w.shape[1]
    w = jnp.zeros((D, LANES), F32).at[:, :E].set(router_w)
    b = jnp.zeros((1, LANES), F32).at[0, :E].set(router_b)
    return pl.pallas_call(
        functools.partial(_router_kernel, n_experts=E), grid=(S // tm,),
        in_specs=[pl.BlockSpec((tm, D), lambda i: (i, 0)),
                  pl.BlockSpec((D, LANES), lambda i: (0, 0)),
                  pl.BlockSpec((1, LANES), lambda i: (0, 0))],
        out_specs=[pl.BlockSpec((tm, LANES), lambda i: (i, 0)),
                   pl.BlockSpec((8, LANES), lambda i: (0, 0))],
        out_shape=[jax.ShapeDtypeStruct((S, LANES), F32), jax.ShapeDtypeStruct((8, LANES), F32)],
        scratch_shapes=[pltpu.VMEM((8, LANES), F32)],
        compiler_params=_params(1), name="l1_router")(x, w, b)


def _row_copy(src_hbm, dst_vmem, sem, src_row, dst_row):
    return pltpu.make_async_copy(src_hbm.at[pl.ds(src_row, 1), :], dst_vmem.at[pl.ds(dst_row, 1), :], sem)


def _gather_rows_kernel(row_src_ref, n_used_ref, x_hbm, o_ref, buf_ref, sem):
    t = pl.program_id(0)
    tm = o_ref.shape[0]
    n_used = n_used_ref[0]

    def start_tile(tile, slot):
        def start(pair, carry):
            for queue in range(2):
                r = 2 * pair + queue
                _row_copy(x_hbm, buf_ref.at[slot], sem.at[slot], row_src_ref[tile * tm + r], r).start(priority=queue)
            return carry
        lax.fori_loop(0, tm // 2, start, 0)

    @pl.when(t == 0)
    def _():
        start_tile(0, 0)

    @pl.when(t + 1 < n_used)
    def _():
        start_tile(t + 1, (t + 1) % 2)

    @pl.when(t < n_used)
    def _():
        slot = t % 2

        def wait(r, carry):
            _row_copy(x_hbm, buf_ref.at[slot], sem.at[slot], 0, r).wait()
            return carry
        lax.fori_loop(0, tm, wait, 0)
        o_ref[...] = buf_ref[slot]

    @pl.when(t >= n_used)
    def _():
        o_ref[...] = jnp.zeros_like(o_ref)


def gather_rows(x, row_src, n_used, *, tm):
    P = row_src.shape[0]
    D = x.shape[1]
    grid_spec = pltpu.PrefetchScalarGridSpec(
        num_scalar_prefetch=2, grid=(P // tm,),
        in_specs=[pl.BlockSpec(memory_space=pl.ANY)],
        out_specs=pl.BlockSpec((tm, D), lambda t, rs, nu: (t, 0)),
        scratch_shapes=[pltpu.VMEM((2, tm, D), x.dtype), pltpu.SemaphoreType.DMA((2,))])
    return pl.pallas_call(
        _gather_rows_kernel, grid_spec=grid_spec,
        out_shape=jax.ShapeDtypeStruct((P, D), x.dtype),
        compiler_params=_params(1), name="l1_moe_gather")(row_src, n_used, x)


def _expert_up_tile(xp_ref, w1_ref, w3_ref, o_ref):
    lo, hi = _unpack_bf16_pair(xp_ref[...])
    lo = lo.astype(BF16)
    hi = hi.astype(BF16)
    half = xp_ref.shape[1]

    def mm(w_ref):
        return (jnp.dot(lo, w_ref[:half, :].astype(BF16), preferred_element_type=F32)
                + jnp.dot(hi, w_ref[half:, :].astype(BF16), preferred_element_type=F32))

    o_ref[...] = (jax.nn.silu(mm(w1_ref)) * mm(w3_ref)).astype(o_ref.dtype)


def _expert_down_tile(h_ref, w_ref, o_ref):
    y = jnp.dot(h_ref[...], w_ref[...].astype(BF16), preferred_element_type=F32)
    half = y.shape[1] // 2
    o_ref[...] = _pack_bf16_pair(y[:, :half], y[:, half:])


def _grouped_kernel(te_ref, n_used_ref, *refs, tile_body):
    o_ref = refs[-1]
    used = pl.program_id(1) < n_used_ref[0]

    @pl.when(used)
    def _():
        tile_body(*refs)

    @pl.when(jnp.logical_not(used))
    def _():
        o_ref[...] = jnp.zeros_like(o_ref)


def _grouped_call(tile_body, x, weights, tile_expert, n_used, *, tm, tn, out_cols, out_dtype, name):
    P, xcols = x.shape
    _, K, N = weights[0].shape
    n_col_tiles = N // tn

    def row_map(j, t, te, nu):
        return (jnp.maximum(jnp.minimum(t, nu[0] - 1), 0), 0)

    def out_map(j, t, te, nu):
        return (t, j)

    def w_map(j, t, te, nu):
        return (te[t], 0, j)

    grid_spec = pltpu.PrefetchScalarGridSpec(
        num_scalar_prefetch=2, grid=(n_col_tiles, P // tm),
        in_specs=[pl.BlockSpec((tm, xcols), row_map)] + [pl.BlockSpec((None, K, tn), w_map)] * len(weights),
        out_specs=pl.BlockSpec((tm, out_cols // n_col_tiles), out_map))
    return pl.pallas_call(
        functools.partial(_grouped_kernel, tile_body=tile_body), grid_spec=grid_spec,
        out_shape=jax.ShapeDtypeStruct((P, out_cols), out_dtype),
        compiler_params=_params(2), name=name)(tile_expert, n_used, x, *weights)


def _combine_ln_kernel(pos0_ref, pos1_ref, y_hbm, x_ref, info_ref, g_ref, b_ref, o_ref, buf_ref, sem, *, col_tiles):
    i = pl.program_id(0)
    tm = x_ref.shape[0]

    def start_tile(tile, slot):
        def start(r, carry):
            for choice, pos_ref in enumerate((pos0_ref, pos1_ref)):
                _row_copy(y_hbm, buf_ref.at[slot, choice], sem.at[slot], pos_ref[tile * tm + r], r).start(
                    priority=choice)
            return carry
        lax.fori_loop(0, tm, start, 0)

    @pl.when(i == 0)
    def _():
        start_tile(0, 0)

    @pl.when(i + 1 < pl.num_programs(0))
    def _():
        start_tile(i + 1, (i + 1) % 2)

    slot = i % 2

    def wait(r, carry):
        for choice in range(2):
            _row_copy(y_hbm, buf_ref.at[slot, choice], sem.at[slot], 0, r).wait()
        return carry

    lax.fori_loop(0, tm, wait, 0)
    info = info_ref[...]
    g0 = info[:, INFO_GATE0:INFO_GATE0 + 1]
    g1 = info[:, INFO_GATE1:INFO_GATE1 + 1]
    lo0, hi0 = _unpack_bf16_pair(buf_ref[slot, 0])
    lo1, hi1 = _unpack_bf16_pair(buf_ref[slot, 1])
    lo = g0 * lo0 + g1 * lo1
    hi = g0 * hi0 + g1 * hi1
    w = lo.shape[1] // col_tiles
    y = jnp.concatenate([part[:, j * w:(j + 1) * w] for j in range(col_tiles) for part in (lo, hi)], axis=1)
    o_ref[...] = _layer_norm_rows(ALPHA * x_ref[...] + y, g_ref[...], b_ref[...])


def combine_ln(y_rows, pos0, pos1, info, x, g, b, *, tm, col_tiles):
    S, D = x.shape
    half = y_rows.shape[1]
    grid_spec = pltpu.PrefetchScalarGridSpec(
        num_scalar_prefetch=2, grid=(S // tm,),
        in_specs=[pl.BlockSpec(memory_space=pl.ANY),
                  pl.BlockSpec((tm, D), lambda i, p0, p1: (i, 0)),
                  pl.BlockSpec((tm, LANES), lambda i, p0, p1: (i, 0)),
                  pl.BlockSpec((1, D), lambda i, p0, p1: (0, 0)),
                  pl.BlockSpec((1, D), lambda i, p0, p1: (0, 0))],
        out_specs=pl.BlockSpec((tm, D), lambda i, p0, p1: (i, 0)),
        scratch_shapes=[pltpu.VMEM((2, 2, tm, half), jnp.uint32), pltpu.SemaphoreType.DMA((2,))])
    return pl.pallas_call(
        functools.partial(_combine_ln_kernel, col_tiles=col_tiles), grid_spec=grid_spec,
        out_shape=jax.ShapeDtypeStruct((S, D), F32),
        compiler_params=_params(1), name="l1_moe_combine_ln")(
            pos0, pos1, y_rows, x, info, g.reshape(1, D), b.reshape(1, D))


def moe_layer(x, xp, router_w, router_b, w1, w3, w2, ln_g, ln_b, *, tm_route, tm, tn_up, tn_down, tm_combine):
    S, D = x.shape
    E = router_w.shape[1]
    info, counts = router(x, router_w, router_b, tm=tm_route)
    idx0 = info[:, INFO_IDX0].astype(jnp.int32)
    idx1 = info[:, INFO_IDX1].astype(jnp.int32)
    count = counts[0, :E].astype(jnp.int32)
    tiles = (count + tm - 1) // tm
    tile_end = jnp.cumsum(tiles)
    offset = (tile_end - tiles) * tm
    n_used = tile_end[-1:]
    pos0 = offset[idx0] + info[:, INFO_RANK0].astype(jnp.int32)
    pos1 = offset[idx1] + info[:, INFO_RANK1].astype(jnp.int32)
    n_rows = TOP_K * S + E * tm
    n_tiles = n_rows // tm
    tile_id = jnp.minimum(jnp.arange(n_tiles, dtype=jnp.int32), n_used[0] - 1)
    tile_expert = jnp.sum(tile_id[:, None] >= tile_end[None, :], axis=1).astype(jnp.int32)
    token = jnp.arange(S, dtype=jnp.int32)
    row_src = jnp.zeros((n_rows,), jnp.int32).at[jnp.concatenate([pos0, pos1])].set(jnp.concatenate([token, token]))

    xs = gather_rows(xp, row_src, n_used, tm=tm)
    hs = _grouped_call(_expert_up_tile, xs, (w1, w3), tile_expert, n_used,
                       tm=tm, tn=tn_up, out_cols=w1.shape[2], out_dtype=BF16, name="l1_moe_up")
    ys = _grouped_call(_expert_down_tile, hs, (w2,), tile_expert, n_used,
                       tm=tm, tn=tn_down, out_cols=D // 2, out_dtype=jnp.uint32, name="l1_moe_down")
    return combine_ln(ys, pos0, pos1, info, x, ln_g, ln_b, tm=tm_combine, col_tiles=D // tn_down)


def kernel(x, l0_mix_w_in, l0_conv_w, l0_conv_b, l0_w_q, l0_w_k, l0_w_v, l0_w_gates, l0_b_igate, l0_b_fgate, l0_head_norm_g, l0_skip, l0_mix_w_out, l0_ln1_g, l0_ln1_b, l0_ffn_w1, l0_ffn_w3, l0_ffn_w2, l0_ln2_g, l0_ln2_b, l1_mix_w_in, l1_mix_b_in, l1_sg_norm_g, l1_sg_norm_b, l1_sg_w, l1_sg_b, l1_mix_w_out, l1_ln1_g, l1_ln1_b, l1_router_w, l1_router_b, l1_exp_w1, l1_exp_w3, l1_exp_w2, l1_ln2_g, l1_ln2_b):
    B, S, D = x.shape
    x0 = x.reshape(B * S, D)

    tm_mm = min(MM_ROW_TILE, S)
    tm_ln = min(LN_ROW_TILE, S)
    (xz,) = matmul(x0.astype(BF16), l0_mix_w_in, tm=tm_mm, tn=MM_COL_TILE, name="l0_in_proj")
    xc, q, k, v, gates = conv_qkv(xz, l0_conv_w, l0_conv_b, l0_w_q, l0_w_k, l0_w_v, l0_w_gates,
                                  tm=min(CONV_ROW_TILE, S), tc=min(CONV_CHANNEL_TILE, l0_conv_w.shape[1]))
    hg, l0_w_out_b = mlstm(q, k, v, xz, xc, gates, l0_b_igate, l0_b_fgate, l0_head_norm_g, l0_skip,
                           chunk=MLSTM_KERNEL_CHUNK, heads_per_step=MLSTM_HEADS_PER_STEP, to_round=l0_mix_w_out)
    x1, x1b = matmul_residual_ln(hg, l0_w_out_b, x0, l0_ln1_g, l0_ln1_b,
                                 tm=tm_ln, max_tk=LN_K_TILE, packed=False, name="l0_out_proj_ln")
    h, l0_w2_b = matmul_swiglu(x1b, l0_ffn_w1, l0_ffn_w3, tm=tm_mm, tn=SWIGLU_COL_TILE, name="l0_ffn_up",
                               to_round=l0_ffn_w2)
    x2, x2b = matmul_residual_ln(h, l0_w2_b, x1, l0_ln2_g, l0_ln2_b,
                                 tm=tm_ln, max_tk=LN_K_TILE, packed=False, name="l0_ffn_down_ln")
    uv, l1_w_out_b = matmul(x2b, l1_mix_w_in, l1_mix_b_in, tm=tm_mm, tn=MM_COL_TILE, name="l1_in_proj_gelu",
                            to_round=l1_mix_w_out)
    gated = spatial_gate(uv, l1_sg_norm_g, l1_sg_norm_b, l1_sg_w, l1_sg_b)
    x3, x3p = matmul_residual_ln(gated, l1_w_out_b, x2, l1_ln1_g, l1_ln1_b,
                                 tm=tm_ln, max_tk=LN_K_TILE, packed=True, name="l1_out_proj_ln")
    y = moe_layer(x3, x3p, l1_router_w, l1_router_b, l1_exp_w1, l1_exp_w3, l1_exp_w2, l1_ln2_g, l1_ln2_b,
                  tm_route=min(ROUTER_ROW_TILE, S), tm=EXPERT_ROW_TILE,
                  tn_up=min(EXPERT_UP_COL_TILE, l1_exp_w1.shape[2]), tn_down=min(EXPERT_DOWN_COL_TILE, D),
                  tm_combine=min(COMBINE_ROW_TILE, S))
    return y.reshape(B, S, D)
```

```python
import functools

import jax
import jax.numpy as jnp
from jax import lax
from jax.experimental import pallas as pl
from jax.experimental.pallas import tpu as pltpu

F32 = jnp.float32
BF16 = jnp.bfloat16

MLSTM_HEADS = 8
MLSTM_QKV_BLOCK = 4
MLSTM_CONV_WIDTH = 4
SG_CHUNK = 128
SG_GROUPS = 8
N_EXPERTS = 8
TOP_K = 2
DEPTH = 2
ALPHA = (2 * DEPTH) ** 0.25
LN_EPS = 1e-5

LANES = 128
BF16_SUBLANES = 16
MXU_DIM = 256
VMEM_LIMIT_BYTES = 56 * 1024 * 1024

MLSTM_KERNEL_CHUNK = 256
MLSTM_HEADS_PER_STEP = 2

MM_ROW_TILE = 1024
MM_COL_TILE = 1024
SWIGLU_COL_TILE = 512
LN_ROW_TILE = 512
LN_K_TILE = 1792
CONV_ROW_TILE = 512
CONV_CHANNEL_TILE = 1024
ROUTER_ROW_TILE = 512
EXPERT_ROW_TILE = 512
EXPERT_UP_COL_TILE = 512
EXPERT_DOWN_COL_TILE = 1024
COMBINE_ROW_TILE = 256


def _params(n_axes):
    return pltpu.CompilerParams(dimension_semantics=("arbitrary",) * n_axes,
                                vmem_limit_bytes=VMEM_LIMIT_BYTES)


def _layer_norm_rows(y, g, b):
    mu = jnp.mean(y, axis=-1, keepdims=True)
    d = y - mu
    var = jnp.mean(d * d, axis=-1, keepdims=True)
    return d * lax.rsqrt(var + LN_EPS) * g + b


def _pack_bf16_pair(lo, hi):
    lo_bits = lax.bitcast_convert_type(lo.astype(BF16).astype(F32), jnp.uint32) >> 16
    hi_bits = lax.bitcast_convert_type(hi.astype(BF16).astype(F32), jnp.uint32) & jnp.uint32(0xFFFF0000)
    return hi_bits | lo_bits


def _unpack_bf16_pair(p):
    lo = lax.bitcast_convert_type(p << 16, F32)
    hi = lax.bitcast_convert_type(p & jnp.uint32(0xFFFF0000), F32)
    return lo, hi


def _call_with_rounding_job(body, grid, in_specs, out_specs, out_shape, args, to_round, **call_kwargs):
    if to_round is None:
        return pl.pallas_call(body, grid=grid, in_specs=in_specs, out_specs=out_specs, out_shape=out_shape,
                              **call_kwargs)(*args)
    rows, cols = to_round.shape
    steps = grid[0] * grid[1]
    slab = rows // steps
    assert rows % steps == 0 and slab % BF16_SUBLANES == 0, (to_round.shape, grid)
    slab_spec = pl.BlockSpec((slab, cols), lambda a, b: (a * grid[1] + b, 0))
    n_in, n_out = len(in_specs), len(out_specs)

    def body_and_round(*refs):
        src_ref = refs[n_in]
        dst_ref = refs[n_in + 1 + n_out]
        dst_ref[...] = src_ref[...].astype(BF16)
        body(*refs[:n_in], *refs[n_in + 1:n_in + 1 + n_out], *refs[n_in + 2 + n_out:])

    return pl.pallas_call(
        body_and_round, grid=grid, in_specs=list(in_specs) + [slab_spec],
        out_specs=list(out_specs) + [slab_spec],
        out_shape=list(out_shape) + [jax.ShapeDtypeStruct((rows, cols), BF16)],
        **call_kwargs)(*args, to_round)


def _mm_kernel(x_ref, w_ref, o_ref):
    o_ref[...] = jnp.dot(x_ref[...], w_ref[...].astype(BF16), preferred_element_type=F32).astype(o_ref.dtype)


def _mm_bias_gelu_kernel(x_ref, w_ref, b_ref, o_ref):
    y = jnp.dot(x_ref[...], w_ref[...].astype(BF16), preferred_element_type=F32) + b_ref[...]
    o_ref[...] = jax.nn.gelu(y).astype(o_ref.dtype)


def matmul(x, w, bias=None, *, tm, tn, name, to_round=None):
    M, K = x.shape
    N = w.shape[1]
    in_specs = [pl.BlockSpec((tm, K), lambda i, j: (i, 0), pipeline_mode=pl.Buffered(1)),
                pl.BlockSpec((K, tn), lambda i, j: (0, j))]
    args = [x, w]
    body = _mm_kernel
    if bias is not None:
        in_specs.append(pl.BlockSpec((1, tn), lambda i, j: (0, j)))
        args.append(bias.reshape(1, N).astype(F32))
        body = _mm_bias_gelu_kernel
    return _call_with_rounding_job(
        body, (M // tm, N // tn), in_specs, [pl.BlockSpec((tm, tn), lambda i, j: (i, j))],
        [jax.ShapeDtypeStruct((M, N), BF16)], args, to_round, compiler_params=_params(2), name=name)


def _mm_swiglu_kernel(x_ref, w1_ref, w3_ref, o_ref):
    x = x_ref[...]
    a = jnp.dot(x, w1_ref[...].astype(BF16), preferred_element_type=F32)
    b = jnp.dot(x, w3_ref[...].astype(BF16), preferred_element_type=F32)
    o_ref[...] = (jax.nn.silu(a) * b).astype(o_ref.dtype)


def matmul_swiglu(x, w1, w3, *, tm, tn, name, to_round=None):
    M, K = x.shape
    N = w1.shape[1]
    wspec = pl.BlockSpec((K, tn), lambda i, j: (0, j))
    return _call_with_rounding_job(
        _mm_swiglu_kernel, (M // tm, N // tn),
        [pl.BlockSpec((tm, K), lambda i, j: (i, 0), pipeline_mode=pl.Buffered(1)), wspec, wspec],
        [pl.BlockSpec((tm, tn), lambda i, j: (i, j))], [jax.ShapeDtypeStruct((M, N), BF16)],
        [x, w1, w3], to_round, compiler_params=_params(2), name=name)


def _mm_res_ln_kernel(a_ref, w_ref, r_ref, g_ref, b_ref, o32_ref, o2_ref, acc0_ref, acc1_ref, *, n_row_blocks):
    i = pl.program_id(0)
    k = pl.program_id(1)
    rc, n = o32_ref.shape

    @pl.when(jnp.logical_and(i == 0, k == 0))
    def _():
        acc0_ref[...] = jnp.zeros_like(acc0_ref)
        acc1_ref[...] = jnp.zeros_like(acc1_ref)

    def finish_slice(done_ref):
        rows = pl.ds(pl.multiple_of(k * rc, rc), rc)
        y = ALPHA * r_ref[...] + done_ref[rows, :]
        done_ref[rows, :] = jnp.zeros((rc, n), F32)
        out = _layer_norm_rows(y, g_ref[...], b_ref[...])
        o32_ref[...] = out
        if o2_ref.dtype == jnp.uint32:
            o2_ref[...] = _pack_bf16_pair(out[:, :n // 2], out[:, n // 2:])
        else:
            o2_ref[...] = out.astype(o2_ref.dtype)

    def accumulate(acc_ref):
        acc_ref[...] += jnp.dot(a_ref[...], w_ref[...], preferred_element_type=F32)

    accs = (acc0_ref, acc1_ref)
    interior = jnp.logical_and(i > 0, i < n_row_blocks)

    @pl.when(i == 0)
    def _():
        accumulate(acc0_ref)

    for parity in range(2):
        @pl.when(jnp.logical_and(interior, i % 2 == parity))
        def _():
            finish_slice(accs[1 - parity])
            accumulate(accs[parity])

    @pl.when(i == n_row_blocks)
    def _():
        finish_slice(accs[(n_row_blocks - 1) % 2])


def _ln_k_steps(K, tm, max_tk):
    for nk in (1, 2, 4, 8, 16, 32, 64):
        if K % nk == 0 and (K // nk) % MXU_DIM == 0 and K // nk <= max_tk and tm % (8 * nk) == 0:
            return nk
    raise ValueError(f"no K tiling for K={K}, tm={tm}")


def matmul_residual_ln(a, w, resid, g, b, *, tm, max_tk, packed, name):
    M, K = a.shape
    N = w.shape[1]
    nk = _ln_k_steps(K, tm, max_tk)
    tk = K // nk
    rc = tm // nk
    n_row_blocks = M // tm
    last = n_row_blocks - 1

    def a_map(i, k):
        return (jnp.minimum(i, last), jnp.where(i <= last, k, nk - 1))

    def w_map(i, k):
        return (jnp.where(i <= last, k, nk - 1), 0)

    def slice_map(i, k):
        return (jnp.where(i == 0, 0, (i - 1) * nk + k), 0)

    vec_spec = pl.BlockSpec((1, N), lambda i, k: (0, 0))
    out_shape = [jax.ShapeDtypeStruct((M, N), F32)]
    out_specs = [pl.BlockSpec((rc, N), slice_map)]
    if packed:
        out_shape.append(jax.ShapeDtypeStruct((M, N // 2), jnp.uint32))
        out_specs.append(pl.BlockSpec((rc, N // 2), slice_map))
    else:
        out_shape.append(jax.ShapeDtypeStruct((M, N), BF16))
        out_specs.append(pl.BlockSpec((rc, N), slice_map))
    return pl.pallas_call(
        functools.partial(_mm_res_ln_kernel, n_row_blocks=n_row_blocks), grid=(n_row_blocks + 1, nk),
        in_specs=[pl.BlockSpec((tm, tk), a_map), pl.BlockSpec((tk, N), w_map),
                  pl.BlockSpec((rc, N), slice_map), vec_spec, vec_spec],
        out_specs=out_specs, out_shape=out_shape,
        scratch_shapes=[pltpu.VMEM((tm, N), F32), pltpu.VMEM((tm, N), F32)],
        compiler_params=_params(2), name=name)(a, w, resid, g.reshape(1, N), b.reshape(1, N))


def _shift_rows(cur, halo, s):
    rolled = pltpu.roll(cur, s, 0)
    halo_top = pltpu.roll(halo, s, 0)[:8]
    row = lax.broadcasted_iota(jnp.int32, (8, cur.shape[1]), 0)
    top = jnp.where(row < s, halo_top, rolled[:8])
    return jnp.concatenate([top, rolled[8:]], axis=0)


def _conv_qkv_kernel(cur_ref, halo_ref, cw_ref, cb_ref, wq_ref, wk_ref, wv_ref, wg_ref,
                     xc_ref, q_ref, k_ref, v_ref, gates_ref):
    i = pl.program_id(0)
    c = pl.program_id(1)
    cur_b = cur_ref[...]
    cur = cur_b.astype(F32)
    halo = jnp.where(i > 0, halo_ref[...].astype(F32), 0.0)
    cw = cw_ref[...]
    acc = cw[MLSTM_CONV_WIDTH - 1:MLSTM_CONV_WIDTH, :] * cur + cb_ref[...]
    for s in range(1, MLSTM_CONV_WIDTH):
        j = MLSTM_CONV_WIDTH - 1 - s
        acc = acc + cw[j:j + 1, :] * _shift_rows(cur, halo, s)
    xc_b = jax.nn.silu(acc).astype(BF16)
    xc_ref[...] = xc_b

    def block_diag(x_b, w_ref):
        n = x_b.shape[1] // MXU_DIM
        parts = [jnp.dot(x_b[:, MXU_DIM * j:MXU_DIM * (j + 1)], w_ref[j], preferred_element_type=F32)
                 for j in range(n)]
        return jnp.concatenate(parts, axis=1).astype(BF16)

    q_b = block_diag(xc_b, wq_ref)
    k_b = block_diag(xc_b, wk_ref)
    v_b = block_diag(cur_b, wv_ref)
    q_ref[...] = q_b
    k_ref[...] = k_b
    v_ref[...] = v_b
    part = (jnp.dot(q_b, wg_ref[0], preferred_element_type=F32)
            + jnp.dot(k_b, wg_ref[1], preferred_element_type=F32)
            + jnp.dot(v_b, wg_ref[2], preferred_element_type=F32))

    @pl.when(c == 0)
    def _():
        gates_ref[...] = jnp.zeros_like(gates_ref)

    gates_ref[...] += part


def _expand_block_diag(w):
    nb, blk, _ = w.shape
    per = MXU_DIM // blk
    wt = w.reshape(nb // per, per, blk, blk)
    eye = jnp.eye(per, dtype=w.dtype)
    full = jnp.einsum("tpcd,pq->tpcqd", wt, eye)
    return full.reshape(nb // per, MXU_DIM, MXU_DIM).astype(BF16)


def conv_qkv(xz, conv_w, conv_b, w_q, w_k, w_v, w_gates, *, tm, tc):
    S = xz.shape[0]
    C = conv_w.shape[1]
    ng = w_gates.shape[2]
    wg = jnp.zeros((3, C, LANES), BF16).at[:, :, :ng].set(w_gates.astype(BF16))
    tiles = tc // MXU_DIM
    halo_rows = BF16_SUBLANES
    blk = pl.BlockSpec((tm, tc), lambda i, c: (i, c))
    bd_spec = pl.BlockSpec((tiles, MXU_DIM, MXU_DIM), lambda i, c: (c, 0, 0))
    act = jax.ShapeDtypeStruct((S, C), BF16)
    return pl.pallas_call(
        _conv_qkv_kernel, grid=(S // tm, C // tc),
        in_specs=[blk,
                  pl.BlockSpec((halo_rows, tc), lambda i, c: (jnp.maximum(i * (tm // halo_rows) - 1, 0), c)),
                  pl.BlockSpec((MLSTM_CONV_WIDTH, tc), lambda i, c: (0, c)),
                  pl.BlockSpec((1, tc), lambda i, c: (0, c)),
                  bd_spec, bd_spec, bd_spec,
                  pl.BlockSpec((3, tc, LANES), lambda i, c: (0, c, 0))],
        out_specs=[blk, blk, blk, blk, pl.BlockSpec((tm, LANES), lambda i, c: (i, 0))],
        out_shape=[act, act, act, act, jax.ShapeDtypeStruct((S, LANES), F32)],
        compiler_params=_params(2), name="l0_conv_qkv")(
            xz, xz, conv_w, conv_b.reshape(1, C),
            _expand_block_diag(w_q), _expand_block_diag(w_k), _expand_block_diag(w_v), wg)


def _mlstm_kernel(q_ref, k_ref, v_ref, z_ref, xc_ref, gates_ref, gbias_ref, hng_ref, skip_ref,
                  o_ref, c_ref, cb_ref, n_ref, m_ref, hh_ref, kwt_ref, *, heads):
    heads_per_step = c_ref.shape[0]
    dh = q_ref.shape[1] // heads_per_step

    @pl.when(pl.program_id(1) == 0)
    def _():
        c_ref[...] = jnp.zeros_like(c_ref)
        cb_ref[...] = jnp.zeros_like(cb_ref)
        n_ref[...] = jnp.zeros_like(n_ref)
        m_ref[...] = jnp.zeros_like(m_ref)

    g = gates_ref[...] + gbias_ref[...]
    for i in range(heads_per_step):
        cols = slice(i * dh, (i + 1) * dh)
        _mlstm_head(pl.program_id(0) * heads_per_step + i, heads, g,
                    q_ref.at[:, cols], k_ref.at[:, cols], v_ref.at[:, cols], z_ref.at[:, cols], xc_ref.at[:, cols],
                    hng_ref.at[:, cols], skip_ref.at[:, cols], o_ref.at[:, cols],
                    c_ref.at[i], cb_ref.at[i], n_ref.at[i], m_ref.at[i], hh_ref.at[i], kwt_ref.at[i])


def _mlstm_head(h, heads, g, q_ref, k_ref, v_ref, z_ref, xc_ref, hng_ref, skip_ref,
                o_ref, c_ref, cb_ref, n_ref, m_ref, hh_ref, kwt_ref):
    L, dh = q_ref.shape
    scale = dh ** -0.5
    lane = lax.broadcasted_iota(jnp.int32, g.shape, 1)
    ig = jnp.sum(jnp.where(lane == h, g, 0.0), axis=1, keepdims=True)
    fpre = jnp.sum(jnp.where(lane == heads + h, g, 0.0), axis=1, keepdims=True)
    lf = jnp.minimum(fpre, 0.0) - jnp.log1p(jnp.exp(-jnp.abs(fpre)))

    row = lax.broadcasted_iota(jnp.int32, (L, L), 0)
    col = lax.broadcasted_iota(jnp.int32, (L, L), 1)
    causal = col <= row
    tri = jnp.where(causal, 1.0, 0.0).astype(BF16)
    lf_wide = jnp.broadcast_to(lf, (L, LANES))
    bcum = jnp.zeros((L, LANES), F32)
    rest = lf_wide
    for _ in range(3):
        part = rest.astype(BF16)
        bcum = bcum + jnp.dot(tri, part, preferred_element_type=F32)
        rest = rest - part.astype(F32)
    bcum = bcum[:, :1]
    r_row = jnp.transpose(jnp.broadcast_to(ig - bcum, (L, LANES)))[:1, :]

    m_prev = m_ref[:1, :1]
    dlog = jnp.where(causal, bcum + r_row, -jnp.inf)
    inter_log = bcum + m_prev
    m_t = jnp.maximum(inter_log, jnp.max(dlog, axis=1, keepdims=True))
    dw = jnp.exp(dlog - m_t)
    inter_w = jnp.exp(inter_log - m_t)

    qb = q_ref[...]
    kb = k_ref[...]
    scores = lax.dot_general(qb, kb, (((1,), (1,)), ((), ())), preferred_element_type=F32) * (dw * scale)
    scores_b = scores.astype(BF16)
    qn = jnp.sum(qb.astype(F32) * n_ref[:1, :], axis=1, keepdims=True)
    den = jnp.sum(scores, axis=1, keepdims=True) + inter_w * qn
    inv_den = 1.0 / jnp.maximum(jnp.abs(den), jnp.exp(-m_t))

    width = min(MXU_DIM, dh)
    col_blocks = [slice(j * width, (j + 1) * width) for j in range(dh // width)]
    row_sum = jnp.zeros((L, 1), F32)
    for cols in col_blocks:
        num = (jnp.dot(scores_b, v_ref[:, cols], preferred_element_type=F32)
               + inter_w * jnp.dot(qb, cb_ref[:, cols], preferred_element_type=F32))
        hblk = num * inv_den
        hh_ref[:, cols] = hblk
        row_sum = row_sum + jnp.sum(hblk, axis=1, keepdims=True)

    b_last = bcum[L - 1:L, :]
    wlog = b_last - bcum + ig
    m_new = jnp.maximum(b_last + m_prev, jnp.max(wlog, axis=0, keepdims=True))
    ws = jnp.exp(wlog - m_new)
    cw = jnp.exp(b_last + m_prev - m_new)
    kw = kb.astype(F32) * (ws * scale)
    n_ref[...] = jnp.broadcast_to(cw * n_ref[:1, :] + jnp.sum(kw, axis=0, keepdims=True), n_ref.shape)
    m_ref[...] = jnp.broadcast_to(m_new, m_ref.shape)
    kwt_ref[...] = jnp.transpose(kw).astype(BF16)
    for cols in col_blocks:
        c_new = cw * c_ref[:, cols] + jnp.dot(kwt_ref[...], v_ref[:, cols], preferred_element_type=F32)
        c_ref[:, cols] = c_new
        cb_ref[:, cols] = c_new.astype(BF16)

    mu = row_sum * (1.0 / dh)
    sq_sum = jnp.zeros((L, 1), F32)
    for cols in col_blocks:
        d = hh_ref[:, cols] - mu
        sq_sum = sq_sum + jnp.sum(d * d, axis=1, keepdims=True)
    rstd = lax.rsqrt(sq_sum * (1.0 / dh) + LN_EPS)
    for cols in col_blocks:
        hn = (hh_ref[:, cols] - mu) * rstd * hng_ref[:, cols]
        out = jax.nn.sigmoid(z_ref[:, cols].astype(F32)) * (hn + skip_ref[:, cols] * xc_ref[:, cols].astype(F32))
        o_ref[:, cols] = out.astype(o_ref.dtype)


def mlstm(q, k, v, xz, xc, gates, b_igate, b_fgate, head_norm_g, skip, *, chunk, heads_per_step, to_round=None):
    S, C = q.shape
    heads = b_igate.shape[0]
    dh = C // heads
    hp = heads_per_step
    gbias = jnp.zeros((1, LANES), F32).at[0, :heads].set(b_igate).at[0, heads:2 * heads].set(b_fgate)
    blk = pl.BlockSpec((chunk, hp * dh), lambda p, c: (c, p))
    vec = pl.BlockSpec((1, hp * dh), lambda p, c: (0, p))
    return _call_with_rounding_job(
        functools.partial(_mlstm_kernel, heads=heads), (heads // hp, S // chunk),
        [blk, blk, blk,
         pl.BlockSpec((chunk, hp * dh), lambda p, c: (c, heads // hp + p)),
         blk,
         pl.BlockSpec((chunk, LANES), lambda p, c: (c, 0)),
         pl.BlockSpec((1, LANES), lambda p, c: (0, 0)),
         vec, vec],
        [blk], [jax.ShapeDtypeStruct((S, C), BF16)],
        [q, k, v, xz, xc, gates, gbias, head_norm_g.reshape(1, C), skip.reshape(1, C)], to_round,
        scratch_shapes=[pltpu.VMEM((hp, dh, dh), F32), pltpu.VMEM((hp, dh, dh), BF16),
                        pltpu.VMEM((hp, 8, dh), F32), pltpu.VMEM((hp, 8, LANES), F32),
                        pltpu.VMEM((hp, chunk, dh), F32), pltpu.VMEM((hp, dh, chunk), BF16)],
        compiler_params=_params(2), name="l0_mlstm")


def _spatial_gate_kernel(u_ref, v_ref, ng_ref, nb_ref, ws_ref, bs_ref, o_ref, *, groups):
    T = v_ref.shape[0]
    gd = v_ref.shape[1] // groups
    vn = _layer_norm_rows(v_ref[...].astype(F32), ng_ref[...], nb_ref[...]).astype(BF16)
    row = lax.broadcasted_iota(jnp.int32, (T, T), 0)
    col = lax.broadcasted_iota(jnp.int32, (T, T), 1)
    causal = col <= row
    bs = bs_ref[...]
    for g in range(groups):
        wc = jnp.where(causal, ws_ref[g], 0.0).astype(BF16)
        sv = jnp.dot(wc, vn[:, g * gd:(g + 1) * gd], preferred_element_type=F32) + bs[:, g:g + 1]
        o_ref[:, g * gd:(g + 1) * gd] = (u_ref[:, g * gd:(g + 1) * gd].astype(F32) * sv).astype(o_ref.dtype)


def spatial_gate(uv, norm_g, norm_b, w_s, b_s):
    S = uv.shape[0]
    W = norm_g.shape[0]
    groups, T, _ = w_s.shape
    bs_cols = jnp.zeros((T, LANES), F32).at[:, :groups].set(b_s.T)
    return pl.pallas_call(
        functools.partial(_spatial_gate_kernel, groups=groups), grid=(S // T,),
        in_specs=[pl.BlockSpec((T, W), lambda c: (c, 0)),
                  pl.BlockSpec((T, W), lambda c: (c, 1)),
                  pl.BlockSpec((1, W), lambda c: (0, 0)),
                  pl.BlockSpec((1, W), lambda c: (0, 0)),
                  pl.BlockSpec((groups, T, T), lambda c: (0, 0, 0)),
                  pl.BlockSpec((T, LANES), lambda c: (0, 0))],
        out_specs=pl.BlockSpec((T, W), lambda c: (c, 0)),
        out_shape=jax.ShapeDtypeStruct((S, W), BF16),
        compiler_params=_params(1), name="l1_spatial_gate")(
            uv, uv, norm_g.reshape(1, W), norm_b.reshape(1, W), w_s, bs_cols)


INFO_IDX0, INFO_IDX1, INFO_GATE0, INFO_GATE1, INFO_RANK0, INFO_RANK1 = range(6)


def _router_kernel(x_ref, w_ref, b_ref, info_ref, count_ref, carry_ref, *, n_experts):
    i = pl.program_id(0)
    tm = x_ref.shape[0]

    @pl.when(i == 0)
    def _():
        carry_ref[...] = jnp.zeros_like(carry_ref)

    x = x_ref[...]
    w = w_ref[...]
    x_hi = x.astype(BF16)
    x_lo = (x - x_hi.astype(F32)).astype(BF16)
    w_hi = w.astype(BF16)
    w_lo = (w - w_hi.astype(F32)).astype(BF16)
    logits = (jnp.dot(x_hi, w_hi, preferred_element_type=F32) + jnp.dot(x_lo, w_hi, preferred_element_type=F32)
              + jnp.dot(x_hi, w_lo, preferred_element_type=F32)) + b_ref[...]
    lane = lax.broadcasted_iota(jnp.int32, logits.shape, 1).astype(F32)
    neg_inf = -jnp.inf
    logits = jnp.where(lane < n_experts, logits, neg_inf)
    m0 = jnp.max(logits, axis=1, keepdims=True)
    i0 = jnp.min(jnp.where(logits == m0, lane, float(LANES)), axis=1, keepdims=True)
    rest = jnp.where(lane == i0, neg_inf, logits)
    m1 = jnp.max(rest, axis=1, keepdims=True)
    i1 = jnp.min(jnp.where(rest == m1, lane, float(LANES)), axis=1, keepdims=True)
    e1 = jnp.exp(m1 - m0)
    denom = 1.0 + e1
    g0 = 1.0 / denom
    g1 = e1 / denom
    hot0 = lane == i0
    hot1 = lane == i1
    member = jnp.where(hot0 | hot1, 1.0, 0.0)
    row = lax.broadcasted_iota(jnp.int32, (tm, tm), 0)
    col = lax.broadcasted_iota(jnp.int32, (tm, tm), 1)
    before = jnp.where(col < row, 1.0, 0.0).astype(BF16)
    rank = jnp.dot(before, member.astype(BF16), preferred_element_type=F32) + carry_ref[:1, :]
    r0 = jnp.sum(jnp.where(hot0, rank, 0.0), axis=1, keepdims=True)
    r1 = jnp.sum(jnp.where(hot1, rank, 0.0), axis=1, keepdims=True)
    info = jnp.zeros_like(logits)
    for slot, val in ((INFO_IDX0, i0), (INFO_IDX1, i1), (INFO_GATE0, g0), (INFO_GATE1, g1),
                      (INFO_RANK0, r0), (INFO_RANK1, r1)):
        info = jnp.where(lane == slot, val, info)
    info_ref[...] = info
    total = carry_ref[...] + jnp.sum(member, axis=0, keepdims=True)
    carry_ref[...] = total
    count_ref[...] = total


def router(x, router_w, router_b, *, tm):
    S, D = x.shape
    E = router_w.shape[1]
    w = jnp.zeros((D, LANES), F32).at[:, :E].set(router_w)
    b = jnp.zeros((1, LANES), F32).at[0, :E].set(router_b)
    return pl.pallas_call(
        functools.partial(_router_kernel, n_experts=E), grid=(S // tm,),
        in_specs=[pl.BlockSpec((tm, D), lambda i: (i, 0)),
                  pl.BlockSpec((D, LANES), lambda i: (0, 0)),
                  pl.BlockSpec((1, LANES), lambda i: (0, 0))],
        out_specs=[pl.BlockSpec((tm, LANES), lambda i: (i, 0)),
                   pl.BlockSpec((8, LANES), lambda i: (0, 0))],
        out_shape=[jax.ShapeDtypeStruct((S, LANES), F32), jax.ShapeDtypeStruct((8, LANES), F32)],
        scratch_shapes=[pltpu.VMEM((8, LANES), F32)],
        compiler_params=_params(1), name="l1_router")(x, w, b)


def _row_copy(src_hbm, dst_vmem, sem, src_row, dst_row):
    return pltpu.make_async_copy(src_hbm.at[pl.ds(src_row, 1), :], dst_vmem.at[pl.ds(dst_row, 1), :], sem)


def _gather_rows_kernel(row_src_ref, n_used_ref, x_hbm, o_ref, buf_ref, sem):
    t = pl.program_id(0)
    tm = o_ref.shape[0]
    n_used = n_used_ref[0]

    def start_tile(tile, slot):
        def start(pair, carry):
            for queue in range(2):
                r = 2 * pair + queue
                _row_copy(x_hbm, buf_ref.at[slot], sem.at[slot], row_src_ref[tile * tm + r], r).start(priority=queue)
            return carry
        lax.fori_loop(0, tm // 2, start, 0)

    @pl.when(t == 0)
    def _():
        start_tile(0, 0)

    @pl.when(t + 1 < n_used)
    def _():
        start_tile(t + 1, (t + 1) % 2)

    @pl.when(t < n_used)
    def _():
        slot = t % 2

        def wait(r, carry):
            _row_copy(x_hbm, buf_ref.at[slot], sem.at[slot], 0, r).wait()
            return carry
        lax.fori_loop(0, tm, wait, 0)
        o_ref[...] = buf_ref[slot]

    @pl.when(t >= n_used)
    def _():
        o_ref[...] = jnp.zeros_like(o_ref)


def gather_rows(x, row_src, n_used, *, tm):
    P = row_src.shape[0]
    D = x.shape[1]
    grid_spec = pltpu.PrefetchScalarGridSpec(
        num_scalar_prefetch=2, grid=(P // tm,),
        in_specs=[pl.BlockSpec(memory_space=pl.ANY)],
        out_specs=pl.BlockSpec((tm, D), lambda t, rs, nu: (t, 0)),
        scratch_shapes=[pltpu.VMEM((2, tm, D), x.dtype), pltpu.SemaphoreType.DMA((2,))])
    return pl.pallas_call(
        _gather_rows_kernel, grid_spec=grid_spec,
        out_shape=jax.ShapeDtypeStruct((P, D), x.dtype),
        compiler_params=_params(1), name="l1_moe_gather")(row_src, n_used, x)


def _expert_up_tile(xp_ref, w1_ref, w3_ref, o_ref):
    lo, hi = _unpack_bf16_pair(xp_ref[...])
    lo = lo.astype(BF16)
    hi = hi.astype(BF16)
    half = xp_ref.shape[1]

    def mm(w_ref):
        return (jnp.dot(lo, w_ref[:half, :].astype(BF16), preferred_element_type=F32)
                + jnp.dot(hi, w_ref[half:, :].astype(BF16), preferred_element_type=F32))

    o_ref[...] = (jax.nn.silu(mm(w1_ref)) * mm(w3_ref)).astype(o_ref.dtype)


def _expert_down_tile(h_ref, w_ref, o_ref):
    y = jnp.dot(h_ref[...], w_ref[...].astype(BF16), preferred_element_type=F32)
    half = y.shape[1] // 2
    o_ref[...] = _pack_bf16_pair(y[:, :half], y[:, half:])


def _grouped_kernel(te_ref, n_used_ref, *refs, tile_body):
    o_ref = refs[-1]
    used = pl.program_id(1) < n_used_ref[0]

    @pl.when(used)
    def _():
        tile_body(*refs)

    @pl.when(jnp.logical_not(used))
    def _():
        o_ref[...] = jnp.zeros_like(o_ref)


def _grouped_call(tile_body, x, weights, tile_expert, n_used, *, tm, tn, out_cols, out_dtype, name):
    P, xcols = x.shape
    _, K, N = weights[0].shape
    n_col_tiles = N // tn

    def row_map(j, t, te, nu):
        return (jnp.maximum(jnp.minimum(t, nu[0] - 1), 0), 0)

    def out_map(j, t, te, nu):
        return (t, j)

    def w_map(j, t, te, nu):
        return (te[t], 0, j)

    grid_spec = pltpu.PrefetchScalarGridSpec(
        num_scalar_prefetch=2, grid=(n_col_tiles, P // tm),
        in_specs=[pl.BlockSpec((tm, xcols), row_map)] + [pl.BlockSpec((None, K, tn), w_map)] * len(weights),
        out_specs=pl.BlockSpec((tm, out_cols // n_col_tiles), out_map))
    return pl.pallas_call(
        functools.partial(_grouped_kernel, tile_body=tile_body), grid_spec=grid_spec,
        out_shape=jax.ShapeDtypeStruct((P, out_cols), out_dtype),
        compiler_params=_params(2), name=name)(tile_expert, n_used, x, *weights)


def _combine_ln_kernel(pos0_ref, pos1_ref, y_hbm, x_ref, info_ref, g_ref, b_ref, o_ref, buf_ref, sem, *, col_tiles):
    i = pl.program_id(0)
    tm = x_ref.shape[0]

    def start_tile(tile, slot):
        def start(r, carry):
            for choice, pos_ref in enumerate((pos0_ref, pos1_ref)):
                _row_copy(y_hbm, buf_ref.at[slot, choice], sem.at[slot], pos_ref[tile * tm + r], r).start(
                    priority=choice)
            return carry
        lax.fori_loop(0, tm, start, 0)

    @pl.when(i == 0)
    def _():
        start_tile(0, 0)

    @pl.when(i + 1 < pl.num_programs(0))
    def _():
        start_tile(i + 1, (i + 1) % 2)

    slot = i % 2

    def wait(r, carry):
        for choice in range(2):
            _row_copy(y_hbm, buf_ref.at[slot, choice], sem.at[slot], 0, r).wait()
        return carry

    lax.fori_loop(0, tm, wait, 0)
    info = info_ref[...]
    g0 = info[:, INFO_GATE0:INFO_GATE0 + 1]
    g1 = info[:, INFO_GATE1:INFO_GATE1 + 1]
    lo0, hi0 = _unpack_bf16_pair(buf_ref[slot, 0])
    lo1, hi1 = _unpack_bf16_pair(buf_ref[slot, 1])
    lo = g0 * lo0 + g1 * lo1
    hi = g0 * hi0 + g1 * hi1
    w = lo.shape[1] // col_tiles
    y = jnp.concatenate([part[:, j * w:(j + 1) * w] for j in range(col_tiles) for part in (lo, hi)], axis=1)
    o_ref[...] = _layer_norm_rows(ALPHA * x_ref[...] + y, g_ref[...], b_ref[...])


def combine_ln(y_rows, pos0, pos1, info, x, g, b, *, tm, col_tiles):
    S, D = x.shape
    half = y_rows.shape[1]
    grid_spec = pltpu.PrefetchScalarGridSpec(
        num_scalar_prefetch=2, grid=(S // tm,),
        in_specs=[pl.BlockSpec(memory_space=pl.ANY),
                  pl.BlockSpec((tm, D), lambda i, p0, p1: (i, 0)),
                  pl.BlockSpec((tm, LANES), lambda i, p0, p1: (i, 0)),
                  pl.BlockSpec((1, D), lambda i, p0, p1: (0, 0)),
                  pl.BlockSpec((1, D), lambda i, p0, p1: (0, 0))],
        out_specs=pl.BlockSpec((tm, D), lambda i, p0, p1: (i, 0)),
        scratch_shapes=[pltpu.VMEM((2, 2, tm, half), jnp.uint32), pltpu.SemaphoreType.DMA((2,))])
    return pl.pallas_call(
        functools.partial(_combine_ln_kernel, col_tiles=col_tiles), grid_spec=grid_spec,
        out_shape=jax.ShapeDtypeStruct((S, D), F32),
        compiler_params=_params(1), name="l1_moe_combine_ln")(
            pos0, pos1, y_rows, x, info, g.reshape(1, D), b.reshape(1, D))


def moe_layer(x, xp, router_w, router_b, w1, w3, w2, ln_g, ln_b, *, tm_route, tm, tn_up, tn_down, tm_combine):
    S, D = x.shape
    E = router_w.shape[1]
    info, counts = router(x, router_w, router_b, tm=tm_route)
    idx0 = info[:, INFO_IDX0].astype(jnp.int32)
    idx1 = info[:, INFO_IDX1].astype(jnp.int32)
    count = counts[0, :E].astype(jnp.int32)
    tiles = (count + tm - 1) // tm
    tile_end = jnp.cumsum(tiles)
    offset = (tile_end - tiles) * tm
    n_used = tile_end[-1:]
    pos0 = offset[idx0] + info[:, INFO_RANK0].astype(jnp.int32)
    pos1 = offset[idx1] + info[:, INFO_RANK1].astype(jnp.int32)
    n_rows = TOP_K * S + E * tm
    n_tiles = n_rows // tm
    tile_id = jnp.minimum(jnp.arange(n_tiles, dtype=jnp.int32), n_used[0] - 1)
    tile_expert = jnp.sum(tile_id[:, None] >= tile_end[None, :], axis=1).astype(jnp.int32)
    token = jnp.arange(S, dtype=jnp.int32)
    row_src = jnp.zeros((n_rows,), jnp.int32).at[jnp.concatenate([pos0, pos1])].set(jnp.concatenate([token, token]))

    xs = gather_rows(xp, row_src, n_used, tm=tm)
    hs = _grouped_call(_expert_up_tile, xs, (w1, w3), tile_expert, n_used,
                       tm=tm, tn=tn_up, out_cols=w1.shape[2], out_dtype=BF16, name="l1_moe_up")
    ys = _grouped_call(_expert_down_tile, hs, (w2,), tile_expert, n_used,
                       tm=tm, tn=tn_down, out_cols=D // 2, out_dtype=jnp.uint32, name="l1_moe_down")
    return combine_ln(ys, pos0, pos1, info, x, ln_g, ln_b, tm=tm_combine, col_tiles=D // tn_down)


def kernel(x, l0_mix_w_in, l0_conv_w, l0_conv_b, l0_w_q, l0_w_k, l0_w_v, l0_w_gates, l0_b_igate, l0_b_fgate, l0_head_norm_g, l0_skip, l0_mix_w_out, l0_ln1_g, l0_ln1_b, l0_ffn_w1, l0_ffn_w3, l0_ffn_w2, l0_ln2_g, l0_ln2_b, l1_mix_w_in, l1_mix_b_in, l1_sg_norm_g, l1_sg_norm_b, l1_sg_w, l1_sg_b, l1_mix_w_out, l1_ln1_g, l1_ln1_b, l1_router_w, l1_router_b, l1_exp_w1, l1_exp_w3, l1_exp_w2, l1_ln2_g, l1_ln2_b):
    B, S, D = x.shape
    x0 = x.reshape(B * S, D)

    tm_mm = min(MM_ROW_TILE, S)
    tm_ln = min(LN_ROW_TILE, S)
    (xz,) = matmul(x0.astype(BF16), l0_mix_w_in, tm=tm_mm, tn=MM_COL_TILE, name="l0_in_proj")
    xc, q, k, v, gates = conv_qkv(xz, l0_conv_w, l0_conv_b, l0_w_q, l0_w_k, l0_w_v, l0_w_gates,
                                  tm=min(CONV_ROW_TILE, S), tc=min(CONV_CHANNEL_TILE, l0_conv_w.shape[1]))
    hg, l0_w_out_b = mlstm(q, k, v, xz, xc, gates, l0_b_igate, l0_b_fgate, l0_head_norm_g, l0_skip,
                           chunk=MLSTM_KERNEL_CHUNK, heads_per_step=MLSTM_HEADS_PER_STEP, to_round=l0_mix_w_out)
    x1, x1b = matmul_residual_ln(hg, l0_w_out_b, x0, l0_ln1_g, l0_ln1_b,
                                 tm=tm_ln, max_tk=LN_K_TILE, packed=False, name="l0_out_proj_ln")
    h, l0_w2_b = matmul_swiglu(x1b, l0_ffn_w1, l0_ffn_w3, tm=tm_mm, tn=SWIGLU_COL_TILE, name="l0_ffn_up",
                               to_round=l0_ffn_w2)
    x2, x2b = matmul_residual_ln(h, l0_w2_b, x1, l0_ln2_g, l0_ln2_b,
                                 tm=tm_ln, max_tk=LN_K_TILE, packed=False, name="l0_ffn_down_ln")
    uv, l1_w_out_b = matmul(x2b, l1_mix_w_in, l1_mix_b_in, tm=tm_mm, tn=MM_COL_TILE, name="l1_in_proj_gelu",
                            to_round=l1_mix_w_out)
    gated = spatial_gate(uv, l1_sg_norm_g, l1_sg_norm_b, l1_sg_w, l1_sg_b)
    x3, x3p = matmul_residual_ln(gated, l1_w_out_b, x2, l1_ln1_g, l1_ln1_b,
                                 tm=tm_ln, max_tk=LN_K_TILE, packed=True, name="l1_out_proj_ln")
    y = moe_layer(x3, x3p, l1_router_w, l1_router_b, l1_exp_w1, l1_exp_w3, l1_exp_w2, l1_ln2_g, l1_ln2_b,
                  tm_route=min(ROUTER_ROW_TILE, S), tm=EXPERT_ROW_TILE,
                  tn_up=min(EXPERT_UP_COL_TILE, l1_exp_w1.shape[2]), tn_down=min(EXPERT_DOWN_COL_TILE, D),
                  tm_combine=min(COMBINE_ROW_TILE, S))
    return y.reshape(B, S, D)
```

```python
import functools

import jax
import jax.numpy as jnp
from jax import lax
from jax.experimental import pallas as pl
from jax.experimental.pallas import tpu as pltpu

F32 = jnp.float32
BF16 = jnp.bfloat16

MLSTM_HEADS = 8
MLSTM_QKV_BLOCK = 4
MLSTM_CONV_WIDTH = 4
SG_CHUNK = 128
SG_GROUPS = 8
N_EXPERTS = 8
TOP_K = 2
DEPTH = 2
ALPHA = (2 * DEPTH) ** 0.25
LN_EPS = 1e-5

LANES = 128
BF16_SUBLANES = 16
MXU_DIM = 256
VMEM_LIMIT_BYTES = 56 * 1024 * 1024

MLSTM_KERNEL_CHUNK = 256
MLSTM_HEADS_PER_STEP = 2

MM_ROW_TILE = 1024
MM_COL_TILE = 1024
SWIGLU_COL_TILE = 512
LN_ROW_TILE = 512
LN_K_TILE = 1792
CONV_ROW_TILE = 512
CONV_CHANNEL_TILE = 1024
ROUTER_ROW_TILE = 512
EXPERT_ROW_TILE = 512
EXPERT_UP_COL_TILE = 512
EXPERT_DOWN_COL_TILE = 1024
COMBINE_ROW_TILE = 256


def _params(n_axes):
    return pltpu.CompilerParams(dimension_semantics=("arbitrary",) * n_axes,
                                vmem_limit_bytes=VMEM_LIMIT_BYTES)


def _layer_norm_rows(y, g, b):
    mu = jnp.mean(y, axis=-1, keepdims=True)
    d = y - mu
    var = jnp.mean(d * d, axis=-1, keepdims=True)
    return d * lax.rsqrt(var + LN_EPS) * g + b


def _pack_bf16_pair(lo, hi):
    lo_bits = lax.bitcast_convert_type(lo.astype(BF16).astype(F32), jnp.uint32) >> 16
    hi_bits = lax.bitcast_convert_type(hi.astype(BF16).astype(F32), jnp.uint32) & jnp.uint32(0xFFFF0000)
    return hi_bits | lo_bits


def _unpack_bf16_pair(p):
    lo = lax.bitcast_convert_type(p << 16, F32)
    hi = lax.bitcast_convert_type(p & jnp.uint32(0xFFFF0000), F32)
    return lo, hi


def _call_with_rounding_job(body, grid, in_specs, out_specs, out_shape, args, to_round, **call_kwargs):
    if to_round is None:
        return pl.pallas_call(body, grid=grid, in_specs=in_specs, out_specs=out_specs, out_shape=out_shape,
                              **call_kwargs)(*args)
    rows, cols = to_round.shape
    steps = grid[0] * grid[1]
    slab = rows // steps
    assert rows % steps == 0 and slab % BF16_SUBLANES == 0, (to_round.shape, grid)
    slab_spec = pl.BlockSpec((slab, cols), lambda a, b: (a * grid[1] + b, 0))
    n_in, n_out = len(in_specs), len(out_specs)

    def body_and_round(*refs):
        src_ref = refs[n_in]
        dst_ref = refs[n_in + 1 + n_out]
        dst_ref[...] = src_ref[...].astype(BF16)
        body(*refs[:n_in], *refs[n_in + 1:n_in + 1 + n_out], *refs[n_in + 2 + n_out:])

    return pl.pallas_call(
        body_and_round, grid=grid, in_specs=list(in_specs) + [slab_spec],
        out_specs=list(out_specs) + [slab_spec],
        out_shape=list(out_shape) + [jax.ShapeDtypeStruct((rows, cols), BF16)],
        **call_kwargs)(*args, to_round)


def _mm_kernel(x_ref, w_ref, o_ref):
    o_ref[...] = jnp.dot(x_ref[...], w_ref[...].astype(BF16), preferred_element_type=F32).astype(o_ref.dtype)


def _mm_bias_gelu_kernel(x_ref, w_ref, b_ref, o_ref):
    y = jnp.dot(x_ref[...], w_ref[...].astype(BF16), preferred_element_type=F32) + b_ref[...]
    o_ref[...] = jax.nn.gelu(y).astype(o_ref.dtype)


def matmul(x, w, bias=None, *, tm, tn, name, to_round=None):
    M, K = x.shape
    N = w.shape[1]
    in_specs = [pl.BlockSpec((tm, K), lambda i, j: (i, 0), pipeline_mode=pl.Buffered(1)),
                pl.BlockSpec((K, tn), lambda i, j: (0, j))]
    args = [x, w]
    body = _mm_kernel
    if bias is not None:
        in_specs.append(pl.BlockSpec((1, tn), lambda i, j: (0, j)))
        args.append(bias.reshape(1, N).astype(F32))
        body = _mm_bias_gelu_kernel
    return _call_with_rounding_job(
        body, (M // tm, N // tn), in_specs, [pl.BlockSpec((tm, tn), lambda i, j: (i, j))],
        [jax.ShapeDtypeStruct((M, N), BF16)], args, to_round, compiler_params=_params(2), name=name)


def _mm_swiglu_kernel(x_ref, w1_ref, w3_ref, o_ref):
    x = x_ref[...]
    a = jnp.dot(x, w1_ref[...].astype(BF16), preferred_element_type=F32)
    b = jnp.dot(x, w3_ref[...].astype(BF16), preferred_element_type=F32)
    o_ref[...] = (jax.nn.silu(a) * b).astype(o_ref.dtype)


def matmul_swiglu(x, w1, w3, *, tm, tn, name, to_round=None):
    M, K = x.shape
    N = w1.shape[1]
    wspec = pl.BlockSpec((K, tn), lambda i, j: (0, j))
    return _call_with_rounding_job(
        _mm_swiglu_kernel, (M // tm, N // tn),
        [pl.BlockSpec((tm, K), lambda i, j: (i, 0), pipeline_mode=pl.Buffered(1)), wspec, wspec],
        [pl.BlockSpec((tm, tn), lambda i, j: (i, j))], [jax.ShapeDtypeStruct((M, N), BF16)],
        [x, w1, w3], to_round, compiler_params=_params(2), name=name)


def _mm_res_ln_kernel(a_ref, w_ref, r_ref, g_ref, b_ref, o32_ref, o2_ref, acc0_ref, acc1_ref, *, n_row_blocks):
    i = pl.program_id(0)
    k = pl.program_id(1)
    rc, n = o32_ref.shape

    @pl.when(jnp.logical_and(i == 0, k == 0))
    def _():
        acc0_ref[...] = jnp.zeros_like(acc0_ref)
        acc1_ref[...] = jnp.zeros_like(acc1_ref)

    def finish_slice(done_ref):
        rows = pl.ds(pl.multiple_of(k * rc, rc), rc)
        y = ALPHA * r_ref[...] + done_ref[rows, :]
        done_ref[rows, :] = jnp.zeros((rc, n), F32)
        out = _layer_norm_rows(y, g_ref[...], b_ref[...])
        o32_ref[...] = out
        if o2_ref.dtype == jnp.uint32:
            o2_ref[...] = _pack_bf16_pair(out[:, :n // 2], out[:, n // 2:])
        else:
            o2_ref[...] = out.astype(o2_ref.dtype)

    def accumulate(acc_ref):
        acc_ref[...] += jnp.dot(a_ref[...], w_ref[...], preferred_element_type=F32)

    accs = (acc0_ref, acc1_ref)
    interior = jnp.logical_and(i > 0, i < n_row_blocks)

    @pl.when(i == 0)
    def _():
        accumulate(acc0_ref)

    for parity in range(2):
        @pl.when(jnp.logical_and(interior, i % 2 == parity))
        def _():
            finish_slice(accs[1 - parity])
            accumulate(accs[parity])

    @pl.when(i == n_row_blocks)
    def _():
        finish_slice(accs[(n_row_blocks - 1) % 2])


def _ln_k_steps(K, tm, max_tk):
    for nk in (1, 2, 4, 8, 16, 32, 64):
        if K % nk == 0 and (K // nk) % MXU_DIM == 0 and K // nk <= max_tk and tm % (8 * nk) == 0:
            return nk
    raise ValueError(f"no K tiling for K={K}, tm={tm}")


def matmul_residual_ln(a, w, resid, g, b, *, tm, max_tk, packed, name):
    M, K = a.shape
    N = w.shape[1]
    nk = _ln_k_steps(K, tm, max_tk)
    tk = K // nk
    rc = tm // nk
    n_row_blocks = M // tm
    last = n_row_blocks - 1

    def a_map(i, k):
        return (jnp.minimum(i, last), jnp.where(i <= last, k, nk - 1))

    def w_map(i, k):
        return (jnp.where(i <= last, k, nk - 1), 0)

    def slice_map(i, k):
        return (jnp.where(i == 0, 0, (i - 1) * nk + k), 0)

    vec_spec = pl.BlockSpec((1, N), lambda i, k: (0, 0))
    out_shape = [jax.ShapeDtypeStruct((M, N), F32)]
    out_specs = [pl.BlockSpec((rc, N), slice_map)]
    if packed:
        out_shape.append(jax.ShapeDtypeStruct((M, N // 2), jnp.uint32))
        out_specs.append(pl.BlockSpec((rc, N // 2), slice_map))
    else:
        out_shape.append(jax.ShapeDtypeStruct((M, N), BF16))
        out_specs.append(pl.BlockSpec((rc, N), slice_map))
    return pl.pallas_call(
        functools.partial(_mm_res_ln_kernel, n_row_blocks=n_row_blocks), grid=(n_row_blocks + 1, nk),
        in_specs=[pl.BlockSpec((tm, tk), a_map), pl.BlockSpec((tk, N), w_map),
                  pl.BlockSpec((rc, N), slice_map), vec_spec, vec_spec],
        out_specs=out_specs, out_shape=out_shape,
        scratch_shapes=[pltpu.VMEM((tm, N), F32), pltpu.VMEM((tm, N), F32)],
        compiler_params=_params(2), name=name)(a, w, resid, g.reshape(1, N), b.reshape(1, N))


def _shift_rows(cur, halo, s):
    rolled = pltpu.roll(cur, s, 0)
    halo_top = pltpu.roll(halo, s, 0)[:8]
    row = lax.broadcasted_iota(jnp.int32, (8, cur.shape[1]), 0)
    top = jnp.where(row < s, halo_top, rolled[:8])
    return jnp.concatenate([top, rolled[8:]], axis=0)


def _conv_qkv_kernel(cur_ref, halo_ref, cw_ref, cb_ref, wq_ref, wk_ref, wv_ref, wg_ref,
                     xc_ref, q_ref, k_ref, v_ref, gates_ref):
    i = pl.program_id(0)
    c = pl.program_id(1)
    cur_b = cur_ref[...]
    cur = cur_b.astype(F32)
    halo = jnp.where(i > 0, halo_ref[...].astype(F32), 0.0)
    cw = cw_ref[...]
    acc = cw[MLSTM_CONV_WIDTH - 1:MLSTM_CONV_WIDTH, :] * cur + cb_ref[...]
    for s in range(1, MLSTM_CONV_WIDTH):
        j = MLSTM_CONV_WIDTH - 1 - s
        acc = acc + cw[j:j + 1, :] * _shift_rows(cur, halo, s)
    xc_b = jax.nn.silu(acc).astype(BF16)
    xc_ref[...] = xc_b

    def block_diag(x_b, w_ref):
        n = x_b.shape[1] // MXU_DIM
        parts = [jnp.dot(x_b[:, MXU_DIM * j:MXU_DIM * (j + 1)], w_ref[j], preferred_element_type=F32)
                 for j in range(n)]
        return jnp.concatenate(parts, axis=1).astype(BF16)

    q_b = block_diag(xc_b, wq_ref)
    k_b = block_diag(xc_b, wk_ref)
    v_b = block_diag(cur_b, wv_ref)
    q_ref[...] = q_b
    k_ref[...] = k_b
    v_ref[...] = v_b
    part = (jnp.dot(q_b, wg_ref[0], preferred_element_type=F32)
            + jnp.dot(k_b, wg_ref[1], preferred_element_type=F32)
            + jnp.dot(v_b, wg_ref[2], preferred_element_type=F32))

    @pl.when(c == 0)
    def _():
        gates_ref[...] = jnp.zeros_like(gates_ref)

    gates_ref[...] += part


def _expand_block_diag(w):
    nb, blk, _ = w.shape
    per = MXU_DIM // blk
    wt = w.reshape(nb // per, per, blk, blk)
    eye = jnp.eye(per, dtype=w.dtype)
    full = jnp.einsum("tpcd,pq->tpcqd", wt, eye)
    return full.reshape(nb // per, MXU_DIM, MXU_DIM).astype(BF16)


def conv_qkv(xz, conv_w, conv_b, w_q, w_k, w_v, w_gates, *, tm, tc):
    S = xz.shape[0]
    C = conv_w.shape[1]
    ng = w_gates.shape[2]
    wg = jnp.zeros((3, C, LANES), BF16).at[:, :, :ng].set(w_gates.astype(BF16))
    tiles = tc // MXU_DIM
    halo_rows = BF16_SUBLANES
    blk = pl.BlockSpec((tm, tc), lambda i, c: (i, c))
    bd_spec = pl.BlockSpec((tiles, MXU_DIM, MXU_DIM), lambda i, c: (c, 0, 0))
    act = jax.ShapeDtypeStruct((S, C), BF16)
    return pl.pallas_call(
        _conv_qkv_kernel, grid=(S // tm, C // tc),
        in_specs=[blk,
                  pl.BlockSpec((halo_rows, tc), lambda i, c: (jnp.maximum(i * (tm // halo_rows) - 1, 0), c)),
                  pl.BlockSpec((MLSTM_CONV_WIDTH, tc), lambda i, c: (0, c)),
                  pl.BlockSpec((1, tc), lambda i, c: (0, c)),
                  bd_spec, bd_spec, bd_spec,
                  pl.BlockSpec((3, tc, LANES), lambda i, c: (0, c, 0))],
        out_specs=[blk, blk, blk, blk, pl.BlockSpec((tm, LANES), lambda i, c: (i, 0))],
        out_shape=[act, act, act, act, jax.ShapeDtypeStruct((S, LANES), F32)],
        compiler_params=_params(2), name="l0_conv_qkv")(
            xz, xz, conv_w, conv_b.reshape(1, C),
            _expand_block_diag(w_q), _expand_block_diag(w_k), _expand_block_diag(w_v), wg)


def _mlstm_kernel(q_ref, k_ref, v_ref, z_ref, xc_ref, gates_ref, gbias_ref, hng_ref, skip_ref,
                  o_ref, c_ref, cb_ref, n_ref, m_ref, hh_ref, kwt_ref, *, heads):
    heads_per_step = c_ref.shape[0]
    dh = q_ref.shape[1] // heads_per_step

    @pl.when(pl.program_id(1) == 0)
    def _():
        c_ref[...] = jnp.zeros_like(c_ref)
        cb_ref[...] = jnp.zeros_like(cb_ref)
        n_ref[...] = jnp.zeros_like(n_ref)
        m_ref[...] = jnp.zeros_like(m_ref)

    g = gates_ref[...] + gbias_ref[...]
    for i in range(heads_per_step):
        cols = slice(i * dh, (i + 1) * dh)
        _mlstm_head(pl.program_id(0) * heads_per_step + i, heads, g,
                    q_ref.at[:, cols], k_ref.at[:, cols], v_ref.at[:, cols], z_ref.at[:, cols], xc_ref.at[:, cols],
                    hng_ref.at[:, cols], skip_ref.at[:, cols], o_ref.at[:, cols],
                    c_ref.at[i], cb_ref.at[i], n_ref.at[i], m_ref.at[i], hh_ref.at[i], kwt_ref.at[i])


def _mlstm_head(h, heads, g, q_ref, k_ref, v_ref, z_ref, xc_ref, hng_ref, skip_ref,
                o_ref, c_ref, cb_ref, n_ref, m_ref, hh_ref, kwt_ref):
    L, dh = q_ref.shape
    scale = dh ** -0.5
    lane = lax.broadcasted_iota(jnp.int32, g.shape, 1)
    ig = jnp.sum(jnp.where(lane == h, g, 0.0), axis=1, keepdims=True)
    fpre = jnp.sum(jnp.where(lane == heads + h, g, 0.0), axis=1, keepdims=True)
    lf = jnp.minimum(fpre, 0.0) - jnp.log1p(jnp.exp(-jnp.abs(fpre)))

    row = lax.broadcasted_iota(jnp.int32, (L, L), 0)
    col = lax.broadcasted_iota(jnp.int32, (L, L), 1)
    causal = col <= row
    tri = jnp.where(causal, 1.0, 0.0).astype(BF16)
    lf_wide = jnp.broadcast_to(lf, (L, LANES))
    bcum = jnp.zeros((L, LANES), F32)
    rest = lf_wide
    for _ in range(3):
        part = rest.astype(BF16)
        bcum = bcum + jnp.dot(tri, part, preferred_element_type=F32)
        rest = rest - part.astype(F32)
    bcum = bcum[:, :1]
    r_row = jnp.transpose(jnp.broadcast_to(ig - bcum, (L, LANES)))[:1, :]

    m_prev = m_ref[:1, :1]
    dlog = jnp.where(causal, bcum + r_row, -jnp.inf)
    inter_log = bcum + m_prev
    m_t = jnp.maximum(inter_log, jnp.max(dlog, axis=1, keepdims=True))
    dw = jnp.exp(dlog - m_t)
    inter_w = jnp.exp(inter_log - m_t)

    qb = q_ref[...]
    kb = k_ref[...]
    scores = lax.dot_general(qb, kb, (((1,), (1,)), ((), ())), preferred_element_type=F32) * (dw * scale)
    scores_b = scores.astype(BF16)
    qn = jnp.sum(qb.astype(F32) * n_ref[:1, :], axis=1, keepdims=True)
    den = jnp.sum(scores, axis=1, keepdims=True) + inter_w * qn
    inv_den = 1.0 / jnp.maximum(jnp.abs(den), jnp.exp(-m_t))

    width = min(MXU_DIM, dh)
    col_blocks = [slice(j * width, (j + 1) * width) for j in range(dh // width)]
    row_sum = jnp.zeros((L, 1), F32)
    for cols in col_blocks:
        num = (jnp.dot(scores_b, v_ref[:, cols], preferred_element_type=F32)
               + inter_w * jnp.dot(qb, cb_ref[:, cols], preferred_element_type=F32))
        hblk = num * inv_den
        hh_ref[:, cols] = hblk
        row_sum = row_sum + jnp.sum(hblk, axis=1, keepdims=True)

    b_last = bcum[L - 1:L, :]
    wlog = b_last - bcum + ig
    m_new = jnp.maximum(b_last + m_prev, jnp.max(wlog, axis=0, keepdims=True))
    ws = jnp.exp(wlog - m_new)
    cw = jnp.exp(b_last + m_prev - m_new)
    kw = kb.astype(F32) * (ws * scale)
    n_ref[...] = jnp.broadcast_to(cw * n_ref[:1, :] + jnp.sum(kw, axis=0, keepdims=True), n_ref.shape)
    m_ref[...] = jnp.broadcast_to(m_new, m_ref.shape)
    kwt_ref[...] = jnp.transpose(kw).astype(BF16)
    for cols in col_blocks:
        c_new = cw * c_ref[:, cols] + jnp.dot(kwt_ref[...], v_ref[:, cols], preferred_element_type=F32)
        c_ref[:, cols] = c_new
        cb_ref[:, cols] = c_new.astype(BF16)

    mu = row_sum * (1.0 / dh)
    sq_sum = jnp.zeros((L, 1), F32)
    for cols in col_blocks:
        d = hh_ref[:, cols] - mu
        sq_sum = sq_sum + jnp.sum(d * d, axis=1, keepdims=True)
    rstd = lax.rsqrt(sq_sum * (1.0 / dh) + LN_EPS)
    for cols in col_blocks:
        hn = (hh_ref[:, cols] - mu) * rstd * hng_ref[:, cols]
        out = jax.nn.sigmoid(z_ref[:, cols].astype(F32)) * (hn + skip_ref[:, cols] * xc_ref[:, cols].astype(F32))
        o_ref[:, cols] = out.astype(o_ref.dtype)


def mlstm(q, k, v, xz, xc, gates, b_igate, b_fgate, head_norm_g, skip, *, chunk, heads_per_step, to_round=None):
    S, C = q.shape
    heads = b_igate.shape[0]
    dh = C // heads
    hp = heads_per_step
    gbias = jnp.zeros((1, LANES), F32).at[0, :heads].set(b_igate).at[0, heads:2 * heads].set(b_fgate)
    blk = pl.BlockSpec((chunk, hp * dh), lambda p, c: (c, p))
    vec = pl.BlockSpec((1, hp * dh), lambda p, c: (0, p))
    return _call_with_rounding_job(
        functools.partial(_mlstm_kernel, heads=heads), (heads // hp, S // chunk),
        [blk, blk, blk,
         pl.BlockSpec((chunk, hp * dh), lambda p, c: (c, heads // hp + p)),
         blk,
         pl.BlockSpec((chunk, LANES), lambda p, c: (c, 0)),
         pl.BlockSpec((1, LANES), lambda p, c: (0, 0)),
         vec, vec],
        [blk], [jax.ShapeDtypeStruct((S, C), BF16)],
        [q, k, v, xz, xc, gates, gbias, head_norm_g.reshape(1, C), skip.reshape(1, C)], to_round,
        scratch_shapes=[pltpu.VMEM((hp, dh, dh), F32), pltpu.VMEM((hp, dh, dh), BF16),
                        pltpu.VMEM((hp, 8, dh), F32), pltpu.VMEM((hp, 8, LANES), F32),
                        pltpu.VMEM((hp, chunk, dh), F32), pltpu.VMEM((hp, dh, chunk), BF16)],
        compiler_params=_params(2), name="l0_mlstm")


def _spatial_gate_kernel(u_ref, v_ref, ng_ref, nb_ref, ws_ref, bs_ref, o_ref, *, groups):
    T = v_ref.shape[0]
    gd = v_ref.shape[1] // groups
    vn = _layer_norm_rows(v_ref[...].astype(F32), ng_ref[...], nb_ref[...]).astype(BF16)
    row = lax.broadcasted_iota(jnp.int32, (T, T), 0)
    col = lax.broadcasted_iota(jnp.int32, (T, T), 1)
    causal = col <= row
    bs = bs_ref[...]
    for g in range(groups):
        wc = jnp.where(causal, ws_ref[g], 0.0).astype(BF16)
        sv = jnp.dot(wc, vn[:, g * gd:(g + 1) * gd], preferred_element_type=F32) + bs[:, g:g + 1]
        o_ref[:, g * gd:(g + 1) * gd] = (u_ref[:, g * gd:(g + 1) * gd].astype(F32) * sv).astype(o_ref.dtype)


def spatial_gate(uv, norm_g, norm_b, w_s, b_s):
    S = uv.shape[0]
    W = norm_g.shape[0]
    groups, T, _ = w_s.shape
    bs_cols = jnp.zeros((T, LANES), F32).at[:, :groups].set(b_s.T)
    return pl.pallas_call(
        functools.partial(_spatial_gate_kernel, groups=groups), grid=(S // T,),
        in_specs=[pl.BlockSpec((T, W), lambda c: (c, 0)),
                  pl.BlockSpec((T, W), lambda c: (c, 1)),
                  pl.BlockSpec((1, W), lambda c: (0, 0)),
                  pl.BlockSpec((1, W), lambda c: (0, 0)),
                  pl.BlockSpec((groups, T, T), lambda c: (0, 0, 0)),
                  pl.BlockSpec((T, LANES), lambda c: (0, 0))],
        out_specs=pl.BlockSpec((T, W), lambda c: (c, 0)),
        out_shape=jax.ShapeDtypeStruct((S, W), BF16),
        compiler_params=_params(1), name="l1_spatial_gate")(
            uv, uv, norm_g.reshape(1, W), norm_b.reshape(1, W), w_s, bs_cols)


INFO_IDX0, INFO_IDX1, INFO_GATE0, INFO_GATE1, INFO_RANK0, INFO_RANK1 = range(6)


def _router_kernel(x_ref, w_ref, b_ref, info_ref, count_ref, carry_ref, *, n_experts):
    i = pl.program_id(0)
    tm = x_ref.shape[0]

    @pl.when(i == 0)
    def _():
        carry_ref[...] = jnp.zeros_like(carry_ref)

    x = x_ref[...]
    w = w_ref[...]
    x_hi = x.astype(BF16)
    x_lo = (x - x_hi.astype(F32)).astype(BF16)
    w_hi = w.astype(BF16)
    w_lo = (w - w_hi.astype(F32)).astype(BF16)
    logits = (jnp.dot(x_hi, w_hi, preferred_element_type=F32) + jnp.dot(x_lo, w_hi, preferred_element_type=F32)
              + jnp.dot(x_hi, w_lo, preferred_element_type=F32)) + b_ref[...]
    lane = lax.broadcasted_iota(jnp.int32, logits.shape, 1).astype(F32)
    neg_inf = -jnp.inf
    logits = jnp.where(lane < n_experts, logits, neg_inf)
    m0 = jnp.max(logits, axis=1, keepdims=True)
    i0 = jnp.min(jnp.where(logits == m0, lane, float(LANES)), axis=1, keepdims=True)
    rest = jnp.where(lane == i0, neg_inf, logits)
    m1 = jnp.max(rest, axis=1, keepdims=True)
    i1 = jnp.min(jnp.where(rest == m1, lane, float(LANES)), axis=1, keepdims=True)
    e1 = jnp.exp(m1 - m0)
    denom = 1.0 + e1
    g0 = 1.0 / denom
    g1 = e1 / denom
    hot0 = lane == i0
    hot1 = lane == i1
    member = jnp.where(hot0 | hot1, 1.0, 0.0)
    row = lax.broadcasted_iota(jnp.int32, (tm, tm), 0)
    col = lax.broadcasted_iota(jnp.int32, (tm, tm), 1)
    before = jnp.where(col < row, 1.0, 0.0).astype(BF16)
    rank = jnp.dot(before, member.astype(BF16), preferred_element_type=F32) + carry_ref[:1, :]
    r0 = jnp.sum(jnp.where(hot0, rank, 0.0), axis=1, keepdims=True)
    r1 = jnp.sum(jnp.where(hot1, rank, 0.0), axis=1, keepdims=True)
    info = jnp.zeros_like(logits)
    for slot, val in ((INFO_IDX0, i0), (INFO_IDX1, i1), (INFO_GATE0, g0), (INFO_GATE1, g1),
                      (INFO_RANK0, r0), (INFO_RANK1, r1)):
        info = jnp.where(lane == slot, val, info)
    info_ref[...] = info
    total = carry_ref[...] + jnp.sum(member, axis=0, keepdims=True)
    carry_ref[...] = total
    count_ref[...] = total


def router(x, router_w, router_b, *, tm):
    S, D = x.shape
    E = router_w.shape[1]
    w = jnp.zeros((D, LANES), F32).at[:, :E].set(router_w)
    b = jnp.zeros((1, LANES), F32).at[0, :E].set(router_b)
    return pl.pallas_call(
        functools.partial(_router_kernel, n_experts=E), grid=(S // tm,),
        in_specs=[pl.BlockSpec((tm, D), lambda i: (i, 0)),
                  pl.BlockSpec((D, LANES), lambda i: (0, 0)),
                  pl.BlockSpec((1, LANES), lambda i: (0, 0))],
        out_specs=[pl.BlockSpec((tm, LANES), lambda i: (i, 0)),
                   pl.BlockSpec((8, LANES), lambda i: (0, 0))],
        out_shape=[jax.ShapeDtypeStruct((S, LANES), F32), jax.ShapeDtypeStruct((8, LANES), F32)],
        scratch_shapes=[pltpu.VMEM((8, LANES), F32)],
        compiler_params=_params(1), name="l1_router")(x, w, b)


def _row_copy(src_hbm, dst_vmem, sem, src_row, dst_row):
    return pltpu.make_async_copy(src_hbm.at[pl.ds(src_row, 1), :], dst_vmem.at[pl.ds(dst_row, 1), :], sem)


def _gather_rows_kernel(row_src_ref, n_used_ref, x_hbm, o_ref, buf_ref, sem):
    t = pl.program_id(0)
    tm = o_ref.shape[0]
    n_used = n_used_ref[0]

    def start_tile(tile, slot):
        def start(pair, carry):
            for queue in range(2):
                r = 2 * pair + queue
                _row_copy(x_hbm, buf_ref.at[slot], sem.at[slot], row_src_ref[tile * tm + r], r).start(priority=queue)
            return carry
        lax.fori_loop(0, tm // 2, start, 0)

    @pl.when(t == 0)
    def _():
        start_tile(0, 0)

    @pl.when(t + 1 < n_used)
    def _():
        start_tile(t + 1, (t + 1) % 2)

    @pl.when(t < n_used)
    def _():
        slot = t % 2

        def wait(r, carry):
            _row_copy(x_hbm, buf_ref.at[slot], sem.at[slot], 0, r).wait()
            return carry
        lax.fori_loop(0, tm, wait, 0)
        o_ref[...] = buf_ref[slot]

    @pl.when(t >= n_used)
    def _():
        o_ref[...] = jnp.zeros_like(o_ref)


def gather_rows(x, row_src, n_used, *, tm):
    P = row_src.shape[0]
    D = x.shape[1]
    grid_spec = pltpu.PrefetchScalarGridSpec(
        num_scalar_prefetch=2, grid=(P // tm,),
        in_specs=[pl.BlockSpec(memory_space=pl.ANY)],
        out_specs=pl.BlockSpec((tm, D), lambda t, rs, nu: (t, 0)),
        scratch_shapes=[pltpu.VMEM((2, tm, D), x.dtype), pltpu.SemaphoreType.DMA((2,))])
    return pl.pallas_call(
        _gather_rows_kernel, grid_spec=grid_spec,
        out_shape=jax.ShapeDtypeStruct((P, D), x.dtype),
        compiler_params=_params(1), name="l1_moe_gather")(row_src, n_used, x)


def _expert_up_tile(xp_ref, w1_ref, w3_ref, o_ref):
    lo, hi = _unpack_bf16_pair(xp_ref[...])
    lo = lo.astype(BF16)
    hi = hi.astype(BF16)
    half = xp_ref.shape[1]

    def mm(w_ref):
        return (jnp.dot(lo, w_ref[:half, :].astype(BF16), preferred_element_type=F32)
                + jnp.dot(hi, w_ref[half:, :].astype(BF16), preferred_element_type=F32))

    o_ref[...] = (jax.nn.silu(mm(w1_ref)) * mm(w3_ref)).astype(o_ref.dtype)


def _expert_down_tile(h_ref, w_ref, o_ref):
    y = jnp.dot(h_ref[...], w_ref[...].astype(BF16), preferred_element_type=F32)
    half = y.shape[1] // 2
    o_ref[...] = _pack_bf16_pair(y[:, :half], y[:, half:])


def _grouped_kernel(te_ref, n_used_ref, run_start_ref, next_expert_ref, last_run_ref, x_ref, *refs,
                    n_weights, tn, tile_body):
    w_hbm = refs[:n_weights]
    o_ref, wbuf_ref, sem, slot_ref = refs[n_weights:]
    j = pl.program_id(0)
    t = pl.program_id(1)
    used = t < n_used_ref[0]

    def weight_copies(expert, col_tile, slot):
        return [pltpu.make_async_copy(w_hbm[i].at[expert, :, pl.ds(col_tile * tn, tn)],
                                      wbuf_ref.at[slot, i], sem.at[slot, i]) for i in range(n_weights)]

    @pl.when(jnp.logical_and(j == 0, t == 0))
    def _():
        slot_ref[0] = 1
        for copy in weight_copies(te_ref[0], 0, 0):
            copy.start()

    @pl.when(jnp.logical_and(used, run_start_ref[t] == 1))
    def _():
        slot = 1 - slot_ref[0]
        slot_ref[0] = slot
        for copy in weight_copies(te_ref[t], j, slot):
            copy.wait()
        is_last_run = last_run_ref[t] == 1

        @pl.when(jnp.logical_not(jnp.logical_and(is_last_run, j == pl.num_programs(0) - 1)))
        def _():
            for copy in weight_copies(next_expert_ref[t], j + last_run_ref[t], 1 - slot):
                copy.start()

    @pl.when(used)
    def _():
        slot = slot_ref[0]
        tile_body(x_ref, *[wbuf_ref.at[slot, i] for i in range(n_weights)], o_ref)

    @pl.when(jnp.logical_not(used))
    def _():
        o_ref[...] = jnp.zeros_like(o_ref)


def _grouped_call(tile_body, x, weights, schedule, *, tm, tn, out_cols, out_dtype, name):
    P, xcols = x.shape
    _, K, N = weights[0].shape
    n_col_tiles = N // tn

    def row_map(j, t, te, nu, *_):
        return (jnp.maximum(jnp.minimum(t, nu[0] - 1), 0), 0)

    def out_map(j, t, *_):
        return (t, j)

    grid_spec = pltpu.PrefetchScalarGridSpec(
        num_scalar_prefetch=len(schedule), grid=(n_col_tiles, P // tm),
        in_specs=[pl.BlockSpec((tm, xcols), row_map)] + [pl.BlockSpec(memory_space=pl.ANY)] * len(weights),
        out_specs=pl.BlockSpec((tm, out_cols // n_col_tiles), out_map),
        scratch_shapes=[pltpu.VMEM((2, len(weights), K, tn), F32),
                        pltpu.SemaphoreType.DMA((2, len(weights))),
                        pltpu.SMEM((1,), jnp.int32)])
    return pl.pallas_call(
        functools.partial(_grouped_kernel, n_weights=len(weights), tn=tn, tile_body=tile_body),
        grid_spec=grid_spec, out_shape=jax.ShapeDtypeStruct((P, out_cols), out_dtype),
        compiler_params=_params(2), name=name)(*schedule, x, *weights)


def _combine_ln_kernel(pos0_ref, pos1_ref, y_hbm, x_ref, info_ref, g_ref, b_ref, o_ref, buf_ref, sem, *, col_tiles):
    i = pl.program_id(0)
    tm = x_ref.shape[0]

    def start_tile(tile, slot):
        def start(r, carry):
            for choice, pos_ref in enumerate((pos0_ref, pos1_ref)):
                _row_copy(y_hbm, buf_ref.at[slot, choice], sem.at[slot], pos_ref[tile * tm + r], r).start(
                    priority=choice)
            return carry
        lax.fori_loop(0, tm, start, 0)

    @pl.when(i == 0)
    def _():
        start_tile(0, 0)

    @pl.when(i + 1 < pl.num_programs(0))
    def _():
        start_tile(i + 1, (i + 1) % 2)

    slot = i % 2

    def wait(r, carry):
        for choice in range(2):
            _row_copy(y_hbm, buf_ref.at[slot, choice], sem.at[slot], 0, r).wait()
        return carry

    lax.fori_loop(0, tm, wait, 0)
    info = info_ref[...]
    g0 = info[:, INFO_GATE0:INFO_GATE0 + 1]
    g1 = info[:, INFO_GATE1:INFO_GATE1 + 1]
    lo0, hi0 = _unpack_bf16_pair(buf_ref[slot, 0])
    lo1, hi1 = _unpack_bf16_pair(buf_ref[slot, 1])
    lo = g0 * lo0 + g1 * lo1
    hi = g0 * hi0 + g1 * hi1
    w = lo.shape[1] // col_tiles
    y = jnp.concatenate([part[:, j * w:(j + 1) * w] for j in range(col_tiles) for part in (lo, hi)], axis=1)
    o_ref[...] = _layer_norm_rows(ALPHA * x_ref[...] + y, g_ref[...], b_ref[...])


def combine_ln(y_rows, pos0, pos1, info, x, g, b, *, tm, col_tiles):
    S, D = x.shape
    half = y_rows.shape[1]
    grid_spec = pltpu.PrefetchScalarGridSpec(
        num_scalar_prefetch=2, grid=(S // tm,),
        in_specs=[pl.BlockSpec(memory_space=pl.ANY),
                  pl.BlockSpec((tm, D), lambda i, p0, p1: (i, 0)),
                  pl.BlockSpec((tm, LANES), lambda i, p0, p1: (i, 0)),
                  pl.BlockSpec((1, D), lambda i, p0, p1: (0, 0)),
                  pl.BlockSpec((1, D), lambda i, p0, p1: (0, 0))],
        out_specs=pl.BlockSpec((tm, D), lambda i, p0, p1: (i, 0)),
        scratch_shapes=[pltpu.VMEM((2, 2, tm, half), jnp.uint32), pltpu.SemaphoreType.DMA((2,))])
    return pl.pallas_call(
        functools.partial(_combine_ln_kernel, col_tiles=col_tiles), grid_spec=grid_spec,
        out_shape=jax.ShapeDtypeStruct((S, D), F32),
        compiler_params=_params(1), name="l1_moe_combine_ln")(
            pos0, pos1, y_rows, x, info, g.reshape(1, D), b.reshape(1, D))


def moe_layer(x, xp, router_w, router_b, w1, w3, w2, ln_g, ln_b, *, tm_route, tm, tn_up, tn_down, tm_combine):
    S, D = x.shape
    E = router_w.shape[1]
    info, counts = router(x, router_w, router_b, tm=tm_route)
    idx0 = info[:, INFO_IDX0].astype(jnp.int32)
    idx1 = info[:, INFO_IDX1].astype(jnp.int32)
    count = counts[0, :E].astype(jnp.int32)
    tiles = (count + tm - 1) // tm
    tile_end = jnp.cumsum(tiles)
    offset = (tile_end - tiles) * tm
    n_used = tile_end[-1:]
    pos0 = offset[idx0] + info[:, INFO_RANK0].astype(jnp.int32)
    pos1 = offset[idx1] + info[:, INFO_RANK1].astype(jnp.int32)
    n_rows = TOP_K * S + E * tm
    n_tiles = n_rows // tm
    tile_id = jnp.minimum(jnp.arange(n_tiles, dtype=jnp.int32), n_used[0] - 1)
    tile_expert = jnp.sum(tile_id[:, None] >= tile_end[None, :], axis=1).astype(jnp.int32)
    token = jnp.arange(S, dtype=jnp.int32)
    row_src = jnp.zeros((n_rows,), jnp.int32).at[jnp.concatenate([pos0, pos1])].set(jnp.concatenate([token, token]))

    expert = jnp.arange(E, dtype=jnp.int32)
    later_nonempty = jnp.logical_and(expert[None, :] > expert[:, None], tiles[None, :] > 0)
    next_nonempty = jnp.min(jnp.where(later_nonempty, expert[None, :], E), axis=1)
    next_of_tile = next_nonempty[tile_expert]
    last_run = (next_of_tile == E).astype(jnp.int32)
    next_expert = jnp.where(next_of_tile == E, tile_expert[0], next_of_tile).astype(jnp.int32)
    tile_index = jnp.arange(n_tiles, dtype=jnp.int32)
    changed = jnp.concatenate([jnp.ones((1,), bool), tile_expert[1:] != tile_expert[:-1]])
    run_start = jnp.logical_and(changed, tile_index < n_used[0]).astype(jnp.int32)
    schedule = (tile_expert, n_used, run_start, next_expert, last_run)

    xs = gather_rows(xp, row_src, n_used, tm=tm)
    hs = _grouped_call(_expert_up_tile, xs, (w1, w3), schedule,
                       tm=tm, tn=tn_up, out_cols=w1.shape[2], out_dtype=BF16, name="l1_moe_up")
    ys = _grouped_call(_expert_down_tile, hs, (w2,), schedule,
                       tm=tm, tn=tn_down, out_cols=D // 2, out_dtype=jnp.uint32, name="l1_moe_down")
    return combine_ln(ys, pos0, pos1, info, x, ln_g, ln_b, tm=tm_combine, col_tiles=D // tn_down)


def kernel(x, l0_mix_w_in, l0_conv_w, l0_conv_b, l0_w_q, l0_w_k, l0_w_v, l0_w_gates, l0_b_igate, l0_b_fgate, l0_head_norm_g, l0_skip, l0_mix_w_out, l0_ln1_g, l0_ln1_b, l0_ffn_w1, l0_ffn_w3, l0_ffn_w2, l0_ln2_g, l0_ln2_b, l1_mix_w_in, l1_mix_b_in, l1_sg_norm_g, l1_sg_norm_b, l1_sg_w, l1_sg_b, l1_mix_w_out, l1_ln1_g, l1_ln1_b, l1_router_w, l1_router_b, l1_exp_w1, l1_exp_w3, l1_exp_w2, l1_ln2_g, l1_ln2_b):
    B, S, D = x.shape
    x0 = x.reshape(B * S, D)

    tm_mm = min(MM_ROW_TILE, S)
    tm_ln = min(LN_ROW_TILE, S)
    (xz,) = matmul(x0.astype(BF16), l0_mix_w_in, tm=tm_mm, tn=MM_COL_TILE, name="l0_in_proj")
    xc, q, k, v, gates = conv_qkv(xz, l0_conv_w, l0_conv_b, l0_w_q, l0_w_k, l0_w_v, l0_w_gates,
                                  tm=min(CONV_ROW_TILE, S), tc=min(CONV_CHANNEL_TILE, l0_conv_w.shape[1]))
    hg, l0_w_out_b = mlstm(q, k, v, xz, xc, gates, l0_b_igate, l0_b_fgate, l0_head_norm_g, l0_skip,
                           chunk=MLSTM_KERNEL_CHUNK, heads_per_step=MLSTM_HEADS_PER_STEP, to_round=l0_mix_w_out)
    x1, x1b = matmul_residual_ln(hg, l0_w_out_b, x0, l0_ln1_g, l0_ln1_b,
                                 tm=tm_ln, max_tk=LN_K_TILE, packed=False, name="l0_out_proj_ln")
    h, l0_w2_b = matmul_swiglu(x1b, l0_ffn_w1, l0_ffn_w3, tm=tm_mm, tn=SWIGLU_COL_TILE, name="l0_ffn_up",
                               to_round=l0_ffn_w2)
    x2, x2b = matmul_residual_ln(h, l0_w2_b, x1, l0_ln2_g, l0_ln2_b,
                                 tm=tm_ln, max_tk=LN_K_TILE, packed=False, name="l0_ffn_down_ln")
    uv, l1_w_out_b = matmul(x2b, l1_mix_w_in, l1_mix_b_in, tm=tm_mm, tn=MM_COL_TILE, name="l1_in_proj_gelu",
                            to_round=l1_mix_w_out)
    gated = spatial_gate(uv, l1_sg_norm_g, l1_sg_norm_b, l1_sg_w, l1_sg_b)
    x3, x3p = matmul_residual_ln(gated, l1_w_out_b, x2, l1_ln1_g, l1_ln1_b,
                                 tm=tm_ln, max_tk=LN_K_TILE, packed=True, name="l1_out_proj_ln")
    y = moe_layer(x3, x3p, l1_router_w, l1_router_b, l1_exp_w1, l1_exp_w3, l1_exp_w2, l1_ln2_g, l1_ln2_b,
                  tm_route=min(ROUTER_ROW_TILE, S), tm=EXPERT_ROW_TILE,
                  tn_up=min(EXPERT_UP_COL_TILE, l1_exp_w1.shape[2]), tn_down=min(EXPERT_DOWN_COL_TILE, D),
                  tm_combine=min(COMBINE_ROW_TILE, S))
    return y.reshape(B, S, D)
```

```python
import functools

import jax
import jax.numpy as jnp
from jax import lax
from jax.experimental import pallas as pl
from jax.experimental.pallas import tpu as pltpu

F32 = jnp.float32
BF16 = jnp.bfloat16

MLSTM_HEADS = 8
MLSTM_QKV_BLOCK = 4
MLSTM_CONV_WIDTH = 4
SG_CHUNK = 128
SG_GROUPS = 8
N_EXPERTS = 8
TOP_K = 2
DEPTH = 2
ALPHA = (2 * DEPTH) ** 0.25
LN_EPS = 1e-5

LANES = 128
BF16_SUBLANES = 16
MXU_DIM = 256
VMEM_LIMIT_BYTES = 56 * 1024 * 1024

MLSTM_KERNEL_CHUNK = 256
MLSTM_HEADS_PER_STEP = 2

MM_ROW_TILE = 1024
MM_COL_TILE = 512
SWIGLU_COL_TILE = 256
LN_ROW_TILE = 512
LN_K_TILE = 1792
CONV_ROW_TILE = 512
CONV_CHANNEL_TILE = 1024
ROUTER_ROW_TILE = 512
EXPERT_ROW_TILE = 512
EXPERT_UP_COL_TILE = 512
EXPERT_DOWN_COL_TILE = 1024
COMBINE_ROW_TILE = 256


def _params(n_axes):
    return pltpu.CompilerParams(dimension_semantics=("arbitrary",) * n_axes,
                                vmem_limit_bytes=VMEM_LIMIT_BYTES)


def _layer_norm_rows(y, g, b):
    mu = jnp.mean(y, axis=-1, keepdims=True)
    d = y - mu
    var = jnp.mean(d * d, axis=-1, keepdims=True)
    return d * lax.rsqrt(var + LN_EPS) * g + b


def _pack_bf16_pair(lo, hi):
    lo_bits = lax.bitcast_convert_type(lo.astype(BF16).astype(F32), jnp.uint32) >> 16
    hi_bits = lax.bitcast_convert_type(hi.astype(BF16).astype(F32), jnp.uint32) & jnp.uint32(0xFFFF0000)
    return hi_bits | lo_bits


def _unpack_bf16_pair(p):
    lo = lax.bitcast_convert_type(p << 16, F32)
    hi = lax.bitcast_convert_type(p & jnp.uint32(0xFFFF0000), F32)
    return lo, hi


def _call_with_rounding_job(body, grid, in_specs, out_specs, out_shape, args, to_round, **call_kwargs):
    if to_round is None:
        return pl.pallas_call(body, grid=grid, in_specs=in_specs, out_specs=out_specs, out_shape=out_shape,
                              **call_kwargs)(*args)
    rows, cols = to_round.shape
    steps = grid[0] * grid[1]
    slab = rows // steps
    assert rows % steps == 0 and slab % BF16_SUBLANES == 0, (to_round.shape, grid)
    slab_spec = pl.BlockSpec((slab, cols), lambda a, b: (a * grid[1] + b, 0))
    n_in, n_out = len(in_specs), len(out_specs)

    def body_and_round(*refs):
        src_ref = refs[n_in]
        dst_ref = refs[n_in + 1 + n_out]
        dst_ref[...] = src_ref[...].astype(BF16)
        body(*refs[:n_in], *refs[n_in + 1:n_in + 1 + n_out], *refs[n_in + 2 + n_out:])

    return pl.pallas_call(
        body_and_round, grid=grid, in_specs=list(in_specs) + [slab_spec],
        out_specs=list(out_specs) + [slab_spec],
        out_shape=list(out_shape) + [jax.ShapeDtypeStruct((rows, cols), BF16)],
        **call_kwargs)(*args, to_round)


def _mm_kernel(x_ref, w_ref, o_ref):
    o_ref[...] = jnp.dot(x_ref[...], w_ref[...].astype(BF16), preferred_element_type=F32).astype(o_ref.dtype)


def _mm_bias_gelu_kernel(x_ref, w_ref, b_ref, o_ref):
    y = jnp.dot(x_ref[...], w_ref[...].astype(BF16), preferred_element_type=F32) + b_ref[...]
    o_ref[...] = jax.nn.gelu(y).astype(o_ref.dtype)


def matmul(x, w, bias=None, *, tm, tn, name, to_round=None):
    M, K = x.shape
    N = w.shape[1]
    in_specs = [pl.BlockSpec((tm, K), lambda j, i: (i, 0)),
                pl.BlockSpec((K, tn), lambda j, i: (0, j))]
    args = [x, w]
    body = _mm_kernel
    if bias is not None:
        in_specs.append(pl.BlockSpec((1, tn), lambda j, i: (0, j)))
        args.append(bias.reshape(1, N).astype(F32))
        body = _mm_bias_gelu_kernel
    return _call_with_rounding_job(
        body, (N // tn, M // tm), in_specs, [pl.BlockSpec((tm, tn), lambda j, i: (i, j))],
        [jax.ShapeDtypeStruct((M, N), BF16)], args, to_round, compiler_params=_params(2), name=name)


def _mm_swiglu_kernel(x_ref, w1_ref, w3_ref, o_ref):
    x = x_ref[...]
    a = jnp.dot(x, w1_ref[...].astype(BF16), preferred_element_type=F32)
    b = jnp.dot(x, w3_ref[...].astype(BF16), preferred_element_type=F32)
    o_ref[...] = (jax.nn.silu(a) * b).astype(o_ref.dtype)


def matmul_swiglu(x, w1, w3, *, tm, tn, name, to_round=None):
    M, K = x.shape
    N = w1.shape[1]
    wspec = pl.BlockSpec((K, tn), lambda j, i: (0, j))
    return _call_with_rounding_job(
        _mm_swiglu_kernel, (N // tn, M // tm),
        [pl.BlockSpec((tm, K), lambda j, i: (i, 0)), wspec, wspec],
        [pl.BlockSpec((tm, tn), lambda j, i: (i, j))], [jax.ShapeDtypeStruct((M, N), BF16)],
        [x, w1, w3], to_round, compiler_params=_params(2), name=name)


def _mm_res_ln_kernel(a_ref, w_ref, r_ref, g_ref, b_ref, o32_ref, o2_ref, acc0_ref, acc1_ref, *, n_row_blocks):
    i = pl.program_id(0)
    k = pl.program_id(1)
    rc, n = o32_ref.shape

    @pl.when(jnp.logical_and(i == 0, k == 0))
    def _():
        acc0_ref[...] = jnp.zeros_like(acc0_ref)
        acc1_ref[...] = jnp.zeros_like(acc1_ref)

    def finish_slice(done_ref):
        rows = pl.ds(pl.multiple_of(k * rc, rc), rc)
        y = ALPHA * r_ref[...] + done_ref[rows, :]
        done_ref[rows, :] = jnp.zeros((rc, n), F32)
        out = _layer_norm_rows(y, g_ref[...], b_ref[...])
        o32_ref[...] = out
        if o2_ref.dtype == jnp.uint32:
            o2_ref[...] = _pack_bf16_pair(out[:, :n // 2], out[:, n // 2:])
        else:
            o2_ref[...] = out.astype(o2_ref.dtype)

    def accumulate(acc_ref):
        acc_ref[...] += jnp.dot(a_ref[...], w_ref[...], preferred_element_type=F32)

    accs = (acc0_ref, acc1_ref)
    interior = jnp.logical_and(i > 0, i < n_row_blocks)

    @pl.when(i == 0)
    def _():
        accumulate(acc0_ref)

    for parity in range(2):
        @pl.when(jnp.logical_and(interior, i % 2 == parity))
        def _():
            finish_slice(accs[1 - parity])
            accumulate(accs[parity])

    @pl.when(i == n_row_blocks)
    def _():
        finish_slice(accs[(n_row_blocks - 1) % 2])


def _ln_k_steps(K, tm, max_tk):
    for nk in (1, 2, 4, 8, 16, 32, 64):
        if K % nk == 0 and (K // nk) % MXU_DIM == 0 and K // nk <= max_tk and tm % (8 * nk) == 0:
            return nk
    raise ValueError(f"no K tiling for K={K}, tm={tm}")


def matmul_residual_ln(a, w, resid, g, b, *, tm, max_tk, packed, name):
    M, K = a.shape
    N = w.shape[1]
    nk = _ln_k_steps(K, tm, max_tk)
    tk = K // nk
    rc = tm // nk
    n_row_blocks = M // tm
    last = n_row_blocks - 1

    def a_map(i, k):
        return (jnp.minimum(i, last), jnp.where(i <= last, k, nk - 1))

    def w_map(i, k):
        return (jnp.where(i <= last, k, nk - 1), 0)

    def slice_map(i, k):
        return (jnp.where(i == 0, 0, (i - 1) * nk + k), 0)

    vec_spec = pl.BlockSpec((1, N), lambda i, k: (0, 0))
    out_shape = [jax.ShapeDtypeStruct((M, N), F32)]
    out_specs = [pl.BlockSpec((rc, N), slice_map)]
    if packed:
        out_shape.append(jax.ShapeDtypeStruct((M, N // 2), jnp.uint32))
        out_specs.append(pl.BlockSpec((rc, N // 2), slice_map))
    else:
        out_shape.append(jax.ShapeDtypeStruct((M, N), BF16))
        out_specs.append(pl.BlockSpec((rc, N), slice_map))
    return pl.pallas_call(
        functools.partial(_mm_res_ln_kernel, n_row_blocks=n_row_blocks), grid=(n_row_blocks + 1, nk),
        in_specs=[pl.BlockSpec((tm, tk), a_map), pl.BlockSpec((tk, N), w_map),
                  pl.BlockSpec((rc, N), slice_map), vec_spec, vec_spec],
        out_specs=out_specs, out_shape=out_shape,
        scratch_shapes=[pltpu.VMEM((tm, N), F32), pltpu.VMEM((tm, N), F32)],
        compiler_params=_params(2), name=name)(a, w, resid, g.reshape(1, N), b.reshape(1, N))


def _shift_rows(cur, halo, s):
    rolled = pltpu.roll(cur, s, 0)
    halo_top = pltpu.roll(halo, s, 0)[:8]
    row = lax.broadcasted_iota(jnp.int32, (8, cur.shape[1]), 0)
    top = jnp.where(row < s, halo_top, rolled[:8])
    return jnp.concatenate([top, rolled[8:]], axis=0)


def _conv_qkv_kernel(cur_ref, halo_ref, cw_ref, cb_ref, wq_ref, wk_ref, wv_ref, wg_ref,
                     xc_ref, q_ref, k_ref, v_ref, gates_ref):
    i = pl.program_id(0)
    c = pl.program_id(1)
    cur_b = cur_ref[...]
    cur = cur_b.astype(F32)
    halo = jnp.where(i > 0, halo_ref[...].astype(F32), 0.0)
    cw = cw_ref[...]
    acc = cw[MLSTM_CONV_WIDTH - 1:MLSTM_CONV_WIDTH, :] * cur + cb_ref[...]
    for s in range(1, MLSTM_CONV_WIDTH):
        j = MLSTM_CONV_WIDTH - 1 - s
        acc = acc + cw[j:j + 1, :] * _shift_rows(cur, halo, s)
    xc_b = jax.nn.silu(acc).astype(BF16)
    xc_ref[...] = xc_b

    def block_diag(x_b, w_ref):
        n = x_b.shape[1] // MXU_DIM
        parts = [jnp.dot(x_b[:, MXU_DIM * j:MXU_DIM * (j + 1)], w_ref[j], preferred_element_type=F32)
                 for j in range(n)]
        return jnp.concatenate(parts, axis=1).astype(BF16)

    q_b = block_diag(xc_b, wq_ref)
    k_b = block_diag(xc_b, wk_ref)
    v_b = block_diag(cur_b, wv_ref)
    q_ref[...] = q_b
    k_ref[...] = k_b
    v_ref[...] = v_b
    part = (jnp.dot(q_b, wg_ref[0], preferred_element_type=F32)
            + jnp.dot(k_b, wg_ref[1], preferred_element_type=F32)
            + jnp.dot(v_b, wg_ref[2], preferred_element_type=F32))

    @pl.when(c == 0)
    def _():
        gates_ref[...] = jnp.zeros_like(gates_ref)

    gates_ref[...] += part


def _expand_block_diag(w):
    nb, blk, _ = w.shape
    per = MXU_DIM // blk
    wt = w.reshape(nb // per, per, blk, blk)
    eye = jnp.eye(per, dtype=w.dtype)
    full = jnp.einsum("tpcd,pq->tpcqd", wt, eye)
    return full.reshape(nb // per, MXU_DIM, MXU_DIM).astype(BF16)


def conv_qkv(xz, conv_w, conv_b, w_q, w_k, w_v, w_gates, *, tm, tc):
    S = xz.shape[0]
    C = conv_w.shape[1]
    ng = w_gates.shape[2]
    wg = jnp.zeros((3, C, LANES), BF16).at[:, :, :ng].set(w_gates.astype(BF16))
    tiles = tc // MXU_DIM
    halo_rows = BF16_SUBLANES
    blk = pl.BlockSpec((tm, tc), lambda i, c: (i, c))
    bd_spec = pl.BlockSpec((tiles, MXU_DIM, MXU_DIM), lambda i, c: (c, 0, 0))
    act = jax.ShapeDtypeStruct((S, C), BF16)
    return pl.pallas_call(
        _conv_qkv_kernel, grid=(S // tm, C // tc),
        in_specs=[blk,
                  pl.BlockSpec((halo_rows, tc), lambda i, c: (jnp.maximum(i * (tm // halo_rows) - 1, 0), c)),
                  pl.BlockSpec((MLSTM_CONV_WIDTH, tc), lambda i, c: (0, c)),
                  pl.BlockSpec((1, tc), lambda i, c: (0, c)),
                  bd_spec, bd_spec, bd_spec,
                  pl.BlockSpec((3, tc, LANES), lambda i, c: (0, c, 0))],
        out_specs=[blk, blk, blk, blk, pl.BlockSpec((tm, LANES), lambda i, c: (i, 0))],
        out_shape=[act, act, act, act, jax.ShapeDtypeStruct((S, LANES), F32)],
        compiler_params=_params(2), name="l0_conv_qkv")(
            xz, xz, conv_w, conv_b.reshape(1, C),
            _expand_block_diag(w_q), _expand_block_diag(w_k), _expand_block_diag(w_v), wg)


def _mlstm_kernel(q_ref, k_ref, v_ref, z_ref, xc_ref, gates_ref, gbias_ref, hng_ref, skip_ref,
                  o_ref, c_ref, cb_ref, n_ref, m_ref, hh_ref, kwt_ref, *, heads):
    heads_per_step = c_ref.shape[0]
    dh = q_ref.shape[1] // heads_per_step

    @pl.when(pl.program_id(1) == 0)
    def _():
        c_ref[...] = jnp.zeros_like(c_ref)
        cb_ref[...] = jnp.zeros_like(cb_ref)
        n_ref[...] = jnp.zeros_like(n_ref)
        m_ref[...] = jnp.zeros_like(m_ref)

    g = gates_ref[...] + gbias_ref[...]
    for i in range(heads_per_step):
        cols = slice(i * dh, (i + 1) * dh)
        _mlstm_head(pl.program_id(0) * heads_per_step + i, heads, g,
                    q_ref.at[:, cols], k_ref.at[:, cols], v_ref.at[:, cols], z_ref.at[:, cols], xc_ref.at[:, cols],
                    hng_ref.at[:, cols], skip_ref.at[:, cols], o_ref.at[:, cols],
                    c_ref.at[i], cb_ref.at[i], n_ref.at[i], m_ref.at[i], hh_ref.at[i], kwt_ref.at[i])


def _mlstm_head(h, heads, g, q_ref, k_ref, v_ref, z_ref, xc_ref, hng_ref, skip_ref,
                o_ref, c_ref, cb_ref, n_ref, m_ref, hh_ref, kwt_ref):
    L, dh = q_ref.shape
    scale = dh ** -0.5
    lane = lax.broadcasted_iota(jnp.int32, g.shape, 1)
    ig = jnp.sum(jnp.where(lane == h, g, 0.0), axis=1, keepdims=True)
    fpre = jnp.sum(jnp.where(lane == heads + h, g, 0.0), axis=1, keepdims=True)
    lf = jnp.minimum(fpre, 0.0) - jnp.log1p(jnp.exp(-jnp.abs(fpre)))

    row = lax.broadcasted_iota(jnp.int32, (L, L), 0)
    col = lax.broadcasted_iota(jnp.int32, (L, L), 1)
    causal = col <= row
    tri = jnp.where(causal, 1.0, 0.0).astype(BF16)
    lf_wide = jnp.broadcast_to(lf, (L, LANES))
    bcum = jnp.zeros((L, LANES), F32)
    rest = lf_wide
    for _ in range(3):
        part = rest.astype(BF16)
        bcum = bcum + jnp.dot(tri, part, preferred_element_type=F32)
        rest = rest - part.astype(F32)
    bcum = bcum[:, :1]
    r_row = jnp.transpose(jnp.broadcast_to(ig - bcum, (L, LANES)))[:1, :]

    m_prev = m_ref[:1, :1]
    dlog = jnp.where(causal, bcum + r_row, -jnp.inf)
    inter_log = bcum + m_prev
    m_t = jnp.maximum(inter_log, jnp.max(dlog, axis=1, keepdims=True))
    dw = jnp.exp(dlog - m_t)
    inter_w = jnp.exp(inter_log - m_t)

    qb = q_ref[...]
    kb = k_ref[...]
    scores = lax.dot_general(qb, kb, (((1,), (1,)), ((), ())), preferred_element_type=F32) * (dw * scale)
    scores_b = scores.astype(BF16)
    qn = jnp.sum(qb.astype(F32) * n_ref[:1, :], axis=1, keepdims=True)
    den = jnp.sum(scores, axis=1, keepdims=True) + inter_w * qn
    inv_den = 1.0 / jnp.maximum(jnp.abs(den), jnp.exp(-m_t))

    width = min(MXU_DIM, dh)
    col_blocks = [slice(j * width, (j + 1) * width) for j in range(dh // width)]
    row_sum = jnp.zeros((L, 1), F32)
    for cols in col_blocks:
        num = (jnp.dot(scores_b, v_ref[:, cols], preferred_element_type=F32)
               + inter_w * jnp.dot(qb, cb_ref[:, cols], preferred_element_type=F32))
        hblk = num * inv_den
        hh_ref[:, cols] = hblk
        row_sum = row_sum + jnp.sum(hblk, axis=1, keepdims=True)

    b_last = bcum[L - 1:L, :]
    wlog = b_last - bcum + ig
    m_new = jnp.maximum(b_last + m_prev, jnp.max(wlog, axis=0, keepdims=True))
    ws = jnp.exp(wlog - m_new)
    cw = jnp.exp(b_last + m_prev - m_new)
    kw = kb.astype(F32) * (ws * scale)
    n_ref[...] = jnp.broadcast_to(cw * n_ref[:1, :] + jnp.sum(kw, axis=0, keepdims=True), n_ref.shape)
    m_ref[...] = jnp.broadcast_to(m_new, m_ref.shape)
    kwt_ref[...] = jnp.transpose(kw).astype(BF16)
    for cols in col_blocks:
        c_new = cw * c_ref[:, cols] + jnp.dot(kwt_ref[...], v_ref[:, cols], preferred_element_type=F32)
        c_ref[:, cols] = c_new
        cb_ref[:, cols] = c_new.astype(BF16)

    mu = row_sum * (1.0 / dh)
    sq_sum = jnp.zeros((L, 1), F32)
    for cols in col_blocks:
        d = hh_ref[:, cols] - mu
        sq_sum = sq_sum + jnp.sum(d * d, axis=1, keepdims=True)
    rstd = lax.rsqrt(sq_sum * (1.0 / dh) + LN_EPS)
    for cols in col_blocks:
        hn = (hh_ref[:, cols] - mu) * rstd * hng_ref[:, cols]
        out = jax.nn.sigmoid(z_ref[:, cols].astype(F32)) * (hn + skip_ref[:, cols] * xc_ref[:, cols].astype(F32))
        o_ref[:, cols] = out.astype(o_ref.dtype)


def mlstm(q, k, v, xz, xc, gates, b_igate, b_fgate, head_norm_g, skip, *, chunk, heads_per_step, to_round=None):
    S, C = q.shape
    heads = b_igate.shape[0]
    dh = C // heads
    hp = heads_per_step
    gbias = jnp.zeros((1, LANES), F32).at[0, :heads].set(b_igate).at[0, heads:2 * heads].set(b_fgate)
    blk = pl.BlockSpec((chunk, hp * dh), lambda p, c: (c, p))
    vec = pl.BlockSpec((1, hp * dh), lambda p, c: (0, p))
    return _call_with_rounding_job(
        functools.partial(_mlstm_kernel, heads=heads), (heads // hp, S // chunk),
        [blk, blk, blk,
         pl.BlockSpec((chunk, hp * dh), lambda p, c: (c, heads // hp + p)),
         blk,
         pl.BlockSpec((chunk, LANES), lambda p, c: (c, 0)),
         pl.BlockSpec((1, LANES), lambda p, c: (0, 0)),
         vec, vec],
        [blk], [jax.ShapeDtypeStruct((S, C), BF16)],
        [q, k, v, xz, xc, gates, gbias, head_norm_g.reshape(1, C), skip.reshape(1, C)], to_round,
        scratch_shapes=[pltpu.VMEM((hp, dh, dh), F32), pltpu.VMEM((hp, dh, dh), BF16),
                        pltpu.VMEM((hp, 8, dh), F32), pltpu.VMEM((hp, 8, LANES), F32),
                        pltpu.VMEM((hp, chunk, dh), F32), pltpu.VMEM((hp, dh, chunk), BF16)],
        compiler_params=_params(2), name="l0_mlstm")


def _spatial_gate_kernel(u_ref, v_ref, ng_ref, nb_ref, ws_ref, bs_ref, o_ref, *, groups):
    T = v_ref.shape[0]
    gd = v_ref.shape[1] // groups
    vn = _layer_norm_rows(v_ref[...].astype(F32), ng_ref[...], nb_ref[...]).astype(BF16)
    row = lax.broadcasted_iota(jnp.int32, (T, T), 0)
    col = lax.broadcasted_iota(jnp.int32, (T, T), 1)
    causal = col <= row
    bs = bs_ref[...]
    for g in range(groups):
        wc = jnp.where(causal, ws_ref[g], 0.0).astype(BF16)
        sv = jnp.dot(wc, vn[:, g * gd:(g + 1) * gd], preferred_element_type=F32) + bs[:, g:g + 1]
        o_ref[:, g * gd:(g + 1) * gd] = (u_ref[:, g * gd:(g + 1) * gd].astype(F32) * sv).astype(o_ref.dtype)


def spatial_gate(uv, norm_g, norm_b, w_s, b_s):
    S = uv.shape[0]
    W = norm_g.shape[0]
    groups, T, _ = w_s.shape
    bs_cols = jnp.zeros((T, LANES), F32).at[:, :groups].set(b_s.T)
    return pl.pallas_call(
        functools.partial(_spatial_gate_kernel, groups=groups), grid=(S // T,),
        in_specs=[pl.BlockSpec((T, W), lambda c: (c, 0)),
                  pl.BlockSpec((T, W), lambda c: (c, 1)),
                  pl.BlockSpec((1, W), lambda c: (0, 0)),
                  pl.BlockSpec((1, W), lambda c: (0, 0)),
                  pl.BlockSpec((groups, T, T), lambda c: (0, 0, 0)),
                  pl.BlockSpec((T, LANES), lambda c: (0, 0))],
        out_specs=pl.BlockSpec((T, W), lambda c: (c, 0)),
        out_shape=jax.ShapeDtypeStruct((S, W), BF16),
        compiler_params=_params(1), name="l1_spatial_gate")(
            uv, uv, norm_g.reshape(1, W), norm_b.reshape(1, W), w_s, bs_cols)


INFO_IDX0, INFO_IDX1, INFO_GATE0, INFO_GATE1, INFO_RANK0, INFO_RANK1 = range(6)


def _router_kernel(x_ref, w_ref, b_ref, info_ref, count_ref, carry_ref, *, n_experts):
    i = pl.program_id(0)
    tm = x_ref.shape[0]

    @pl.when(i == 0)
    def _():
        carry_ref[...] = jnp.zeros_like(carry_ref)

    x = x_ref[...]
    w = w_ref[...]
    x_hi = x.astype(BF16)
    x_lo = (x - x_hi.astype(F32)).astype(BF16)
    w_hi = w.astype(BF16)
    w_lo = (w - w_hi.astype(F32)).astype(BF16)
    logits = (jnp.dot(x_hi, w_hi, preferred_element_type=F32) + jnp.dot(x_lo, w_hi, preferred_element_type=F32)
              + jnp.dot(x_hi, w_lo, preferred_element_type=F32)) + b_ref[...]
    lane = lax.broadcasted_iota(jnp.int32, logits.shape, 1).astype(F32)
    neg_inf = -jnp.inf
    logits = jnp.where(lane < n_experts, logits, neg_inf)
    m0 = jnp.max(logits, axis=1, keepdims=True)
    i0 = jnp.min(jnp.where(logits == m0, lane, float(LANES)), axis=1, keepdims=True)
    rest = jnp.where(lane == i0, neg_inf, logits)
    m1 = jnp.max(rest, axis=1, keepdims=True)
    i1 = jnp.min(jnp.where(rest == m1, lane, float(LANES)), axis=1, keepdims=True)
    e1 = jnp.exp(m1 - m0)
    denom = 1.0 + e1
    g0 = 1.0 / denom
    g1 = e1 / denom
    hot0 = lane == i0
    hot1 = lane == i1
    member = jnp.where(hot0 | hot1, 1.0, 0.0)
    row = lax.broadcasted_iota(jnp.int32, (tm, tm), 0)
    col = lax.broadcasted_iota(jnp.int32, (tm, tm), 1)
    before = jnp.where(col < row, 1.0, 0.0).astype(BF16)
    rank = jnp.dot(before, member.astype(BF16), preferred_element_type=F32) + carry_ref[:1, :]
    r0 = jnp.sum(jnp.where(hot0, rank, 0.0), axis=1, keepdims=True)
    r1 = jnp.sum(jnp.where(hot1, rank, 0.0), axis=1, keepdims=True)
    info = jnp.zeros_like(logits)
    for slot, val in ((INFO_IDX0, i0), (INFO_IDX1, i1), (INFO_GATE0, g0), (INFO_GATE1, g1),
                      (INFO_RANK0, r0), (INFO_RANK1, r1)):
        info = jnp.where(lane == slot, val, info)
    info_ref[...] = info
    total = carry_ref[...] + jnp.sum(member, axis=0, keepdims=True)
    carry_ref[...] = total
    count_ref[...] = total


def router(x, router_w, router_b, *, tm):
    S, D = x.shape
    E = router_w.shape[1]
    w = jnp.zeros((D, LANES), F32).at[:, :E].set(router_w)
    b = jnp.zeros((1, LANES), F32).at[0, :E].set(router_b)
    return pl.pallas_call(
        functools.partial(_router_kernel, n_experts=E), grid=(S // tm,),
        in_specs=[pl.BlockSpec((tm, D), lambda i: (i, 0)),
                  pl.BlockSpec((D, LANES), lambda i: (0, 0)),
                  pl.BlockSpec((1, LANES), lambda i: (0, 0))],
        out_specs=[pl.BlockSpec((tm, LANES), lambda i: (i, 0)),
                   pl.BlockSpec((8, LANES), lambda i: (0, 0))],
        out_shape=[jax.ShapeDtypeStruct((S, LANES), F32), jax.ShapeDtypeStruct((8, LANES), F32)],
        scratch_shapes=[pltpu.VMEM((8, LANES), F32)],
        compiler_params=_params(1), name="l1_router")(x, w, b)


def _row_copy(src_hbm, dst_vmem, sem, src_row, dst_row):
    return pltpu.make_async_copy(src_hbm.at[pl.ds(src_row, 1), :], dst_vmem.at[pl.ds(dst_row, 1), :], sem)


def _gather_rows_kernel(row_src_ref, n_used_ref, x_hbm, o_ref, buf_ref, sem):
    t = pl.program_id(0)
    tm = o_ref.shape[0]
    n_used = n_used_ref[0]

    def start_tile(tile, slot):
        def start(pair, carry):
            for queue in range(2):
                r = 2 * pair + queue
                _row_copy(x_hbm, buf_ref.at[slot], sem.at[slot], row_src_ref[tile * tm + r], r).start(priority=queue)
            return carry
        lax.fori_loop(0, tm // 2, start, 0)

    @pl.when(t == 0)
    def _():
        start_tile(0, 0)

    @pl.when(t + 1 < n_used)
    def _():
        start_tile(t + 1, (t + 1) % 2)

    @pl.when(t < n_used)
    def _():
        slot = t % 2

        def wait(r, carry):
            _row_copy(x_hbm, buf_ref.at[slot], sem.at[slot], 0, r).wait()
            return carry
        lax.fori_loop(0, tm, wait, 0)
        o_ref[...] = buf_ref[slot]

    @pl.when(t >= n_used)
    def _():
        o_ref[...] = jnp.zeros_like(o_ref)


def gather_rows(x, row_src, n_used, *, tm):
    P = row_src.shape[0]
    D = x.shape[1]
    grid_spec = pltpu.PrefetchScalarGridSpec(
        num_scalar_prefetch=2, grid=(P // tm,),
        in_specs=[pl.BlockSpec(memory_space=pl.ANY)],
        out_specs=pl.BlockSpec((tm, D), lambda t, rs, nu: (t, 0)),
        scratch_shapes=[pltpu.VMEM((2, tm, D), x.dtype), pltpu.SemaphoreType.DMA((2,))])
    return pl.pallas_call(
        _gather_rows_kernel, grid_spec=grid_spec,
        out_shape=jax.ShapeDtypeStruct((P, D), x.dtype),
        compiler_params=_params(1), name="l1_moe_gather")(row_src, n_used, x)


def _expert_up_tile(xp_ref, w1_ref, w3_ref, o_ref):
    lo, hi = _unpack_bf16_pair(xp_ref[...])
    lo = lo.astype(BF16)
    hi = hi.astype(BF16)
    half = xp_ref.shape[1]

    def mm(w_ref):
        return (jnp.dot(lo, w_ref[:half, :].astype(BF16), preferred_element_type=F32)
                + jnp.dot(hi, w_ref[half:, :].astype(BF16), preferred_element_type=F32))

    o_ref[...] = (jax.nn.silu(mm(w1_ref)) * mm(w3_ref)).astype(o_ref.dtype)


def _expert_down_tile(h_ref, w_ref, o_ref):
    y = jnp.dot(h_ref[...], w_ref[...].astype(BF16), preferred_element_type=F32)
    half = y.shape[1] // 2
    o_ref[...] = _pack_bf16_pair(y[:, :half], y[:, half:])


def _grouped_kernel(te_ref, n_used_ref, run_start_ref, next_expert_ref, last_run_ref, x_ref, *refs,
                    n_weights, tn, tile_body):
    w_hbm = refs[:n_weights]
    o_ref, wbuf_ref, sem, slot_ref = refs[n_weights:]
    j = pl.program_id(0)
    t = pl.program_id(1)
    used = t < n_used_ref[0]

    def weight_copies(expert, col_tile, slot):
        return [pltpu.make_async_copy(w_hbm[i].at[expert, :, pl.ds(col_tile * tn, tn)],
                                      wbuf_ref.at[slot, i], sem.at[slot, i]) for i in range(n_weights)]

    @pl.when(jnp.logical_and(j == 0, t == 0))
    def _():
        slot_ref[0] = 1
        for copy in weight_copies(te_ref[0], 0, 0):
            copy.start()

    @pl.when(jnp.logical_and(used, run_start_ref[t] == 1))
    def _():
        slot = 1 - slot_ref[0]
        slot_ref[0] = slot
        for copy in weight_copies(te_ref[t], j, slot):
            copy.wait()
        is_last_run = last_run_ref[t] == 1

        @pl.when(jnp.logical_not(jnp.logical_and(is_last_run, j == pl.num_programs(0) - 1)))
        def _():
            for copy in weight_copies(next_expert_ref[t], j + last_run_ref[t], 1 - slot):
                copy.start()

    @pl.when(used)
    def _():
        slot = slot_ref[0]
        tile_body(x_ref, *[wbuf_ref.at[slot, i] for i in range(n_weights)], o_ref)

    @pl.when(jnp.logical_not(used))
    def _():
        o_ref[...] = jnp.zeros_like(o_ref)


def _grouped_call(tile_body, x, weights, schedule, *, tm, tn, out_cols, out_dtype, name):
    P, xcols = x.shape
    _, K, N = weights[0].shape
    n_col_tiles = N // tn

    def row_map(j, t, te, nu, *_):
        return (jnp.maximum(jnp.minimum(t, nu[0] - 1), 0), 0)

    def out_map(j, t, *_):
        return (t, j)

    grid_spec = pltpu.PrefetchScalarGridSpec(
        num_scalar_prefetch=len(schedule), grid=(n_col_tiles, P // tm),
        in_specs=[pl.BlockSpec((tm, xcols), row_map)] + [pl.BlockSpec(memory_space=pl.ANY)] * len(weights),
        out_specs=pl.BlockSpec((tm, out_cols // n_col_tiles), out_map),
        scratch_shapes=[pltpu.VMEM((2, len(weights), K, tn), F32),
                        pltpu.SemaphoreType.DMA((2, len(weights))),
                        pltpu.SMEM((1,), jnp.int32)])
    return pl.pallas_call(
        functools.partial(_grouped_kernel, n_weights=len(weights), tn=tn, tile_body=tile_body),
        grid_spec=grid_spec, out_shape=jax.ShapeDtypeStruct((P, out_cols), out_dtype),
        compiler_params=_params(2), name=name)(*schedule, x, *weights)


def _combine_ln_kernel(pos0_ref, pos1_ref, y_hbm, x_ref, info_ref, g_ref, b_ref, o_ref, buf_ref, sem, *, col_tiles):
    i = pl.program_id(0)
    tm = x_ref.shape[0]

    def start_tile(tile, slot):
        def start(r, carry):
            for choice, pos_ref in enumerate((pos0_ref, pos1_ref)):
                _row_copy(y_hbm, buf_ref.at[slot, choice], sem.at[slot], pos_ref[tile * tm + r], r).start(
                    priority=choice)
            return carry
        lax.fori_loop(0, tm, start, 0)

    @pl.when(i == 0)
    def _():
        start_tile(0, 0)

    @pl.when(i + 1 < pl.num_programs(0))
    def _():
        start_tile(i + 1, (i + 1) % 2)

    slot = i % 2

    def wait(r, carry):
        for choice in range(2):
            _row_copy(y_hbm, buf_ref.at[slot, choice], sem.at[slot], 0, r).wait()
        return carry

    lax.fori_loop(0, tm, wait, 0)
    info = info_ref[...]
    g0 = info[:, INFO_GATE0:INFO_GATE0 + 1]
    g1 = info[:, INFO_GATE1:INFO_GATE1 + 1]
    lo0, hi0 = _unpack_bf16_pair(buf_ref[slot, 0])
    lo1, hi1 = _unpack_bf16_pair(buf_ref[slot, 1])
    lo = g0 * lo0 + g1 * lo1
    hi = g0 * hi0 + g1 * hi1
    w = lo.shape[1] // col_tiles
    y = jnp.concatenate([part[:, j * w:(j + 1) * w] for j in range(col_tiles) for part in (lo, hi)], axis=1)
    o_ref[...] = _layer_norm_rows(ALPHA * x_ref[...] + y, g_ref[...], b_ref[...])


def combine_ln(y_rows, pos0, pos1, info, x, g, b, *, tm, col_tiles):
    S, D = x.shape
    half = y_rows.shape[1]
    grid_spec = pltpu.PrefetchScalarGridSpec(
        num_scalar_prefetch=2, grid=(S // tm,),
        in_specs=[pl.BlockSpec(memory_space=pl.ANY),
                  pl.BlockSpec((tm, D), lambda i, p0, p1: (i, 0)),
                  pl.BlockSpec((tm, LANES), lambda i, p0, p1: (i, 0)),
                  pl.BlockSpec((1, D), lambda i, p0, p1: (0, 0)),
                  pl.BlockSpec((1, D), lambda i, p0, p1: (0, 0))],
        out_specs=pl.BlockSpec((tm, D), lambda i, p0, p1: (i, 0)),
        scratch_shapes=[pltpu.VMEM((2, 2, tm, half), jnp.uint32), pltpu.SemaphoreType.DMA((2,))])
    return pl.pallas_call(
        functools.partial(_combine_ln_kernel, col_tiles=col_tiles), grid_spec=grid_spec,
        out_shape=jax.ShapeDtypeStruct((S, D), F32),
        compiler_params=_params(1), name="l1_moe_combine_ln")(
            pos0, pos1, y_rows, x, info, g.reshape(1, D), b.reshape(1, D))


def moe_layer(x, xp, router_w, router_b, w1, w3, w2, ln_g, ln_b, *, tm_route, tm, tn_up, tn_down, tm_combine):
    S, D = x.shape
    E = router_w.shape[1]
    info, counts = router(x, router_w, router_b, tm=tm_route)
    idx0 = info[:, INFO_IDX0].astype(jnp.int32)
    idx1 = info[:, INFO_IDX1].astype(jnp.int32)
    count = counts[0, :E].astype(jnp.int32)
    tiles = (count + tm - 1) // tm
    tile_end = jnp.cumsum(tiles)
    offset = (tile_end - tiles) * tm
    n_used = tile_end[-1:]
    pos0 = offset[idx0] + info[:, INFO_RANK0].astype(jnp.int32)
    pos1 = offset[idx1] + info[:, INFO_RANK1].astype(jnp.int32)
    n_rows = TOP_K * S + E * tm
    n_tiles = n_rows // tm
    tile_id = jnp.minimum(jnp.arange(n_tiles, dtype=jnp.int32), n_used[0] - 1)
    tile_expert = jnp.sum(tile_id[:, None] >= tile_end[None, :], axis=1).astype(jnp.int32)
    token = jnp.arange(S, dtype=jnp.int32)
    row_src = jnp.zeros((n_rows,), jnp.int32).at[jnp.concatenate([pos0, pos1])].set(jnp.concatenate([token, token]))

    expert = jnp.arange(E, dtype=jnp.int32)
    later_nonempty = jnp.logical_and(expert[None, :] > expert[:, None], tiles[None, :] > 0)
    next_nonempty = jnp.min(jnp.where(later_nonempty, expert[None, :], E), axis=1)
    next_of_tile = next_nonempty[tile_expert]
    last_run = (next_of_tile == E).astype(jnp.int32)
    next_expert = jnp.where(next_of_tile == E, tile_expert[0], next_of_tile).astype(jnp.int32)
    tile_index = jnp.arange(n_tiles, dtype=jnp.int32)
    changed = jnp.concatenate([jnp.ones((1,), bool), tile_expert[1:] != tile_expert[:-1]])
    run_start = jnp.logical_and(changed, tile_index < n_used[0]).astype(jnp.int32)
    schedule = (tile_expert, n_used, run_start, next_expert, last_run)

    xs = gather_rows(xp, row_src, n_used, tm=tm)
    hs = _grouped_call(_expert_up_tile, xs, (w1, w3), schedule,
                       tm=tm, tn=tn_up, out_cols=w1.shape[2], out_dtype=BF16, name="l1_moe_up")
    ys = _grouped_call(_expert_down_tile, hs, (w2,), schedule,
                       tm=tm, tn=tn_down, out_cols=D // 2, out_dtype=jnp.uint32, name="l1_moe_down")
    return combine_ln(ys, pos0, pos1, info, x, ln_g, ln_b, tm=tm_combine, col_tiles=D // tn_down)


def kernel(x, l0_mix_w_in, l0_conv_w, l0_conv_b, l0_w_q, l0_w_k, l0_w_v, l0_w_gates, l0_b_igate, l0_b_fgate, l0_head_norm_g, l0_skip, l0_mix_w_out, l0_ln1_g, l0_ln1_b, l0_ffn_w1, l0_ffn_w3, l0_ffn_w2, l0_ln2_g, l0_ln2_b, l1_mix_w_in, l1_mix_b_in, l1_sg_norm_g, l1_sg_norm_b, l1_sg_w, l1_sg_b, l1_mix_w_out, l1_ln1_g, l1_ln1_b, l1_router_w, l1_router_b, l1_exp_w1, l1_exp_w3, l1_exp_w2, l1_ln2_g, l1_ln2_b):
    B, S, D = x.shape
    x0 = x.reshape(B * S, D)

    tm_mm = min(MM_ROW_TILE, S)
    tm_ln = min(LN_ROW_TILE, S)
    (xz,) = matmul(x0.astype(BF16), l0_mix_w_in, tm=tm_mm, tn=MM_COL_TILE, name="l0_in_proj")
    xc, q, k, v, gates = conv_qkv(xz, l0_conv_w, l0_conv_b, l0_w_q, l0_w_k, l0_w_v, l0_w_gates,
                                  tm=min(CONV_ROW_TILE, S), tc=min(CONV_CHANNEL_TILE, l0_conv_w.shape[1]))
    hg, l0_w_out_b = mlstm(q, k, v, xz, xc, gates, l0_b_igate, l0_b_fgate, l0_head_norm_g, l0_skip,
                           chunk=MLSTM_KERNEL_CHUNK, heads_per_step=MLSTM_HEADS_PER_STEP, to_round=l0_mix_w_out)
    x1, x1b = matmul_residual_ln(hg, l0_w_out_b, x0, l0_ln1_g, l0_ln1_b,
                                 tm=tm_ln, max_tk=LN_K_TILE, packed=False, name="l0_out_proj_ln")
    h, l0_w2_b = matmul_swiglu(x1b, l0_ffn_w1, l0_ffn_w3, tm=tm_mm, tn=SWIGLU_COL_TILE, name="l0_ffn_up",
                               to_round=l0_ffn_w2)
    x2, x2b = matmul_residual_ln(h, l0_w2_b, x1, l0_ln2_g, l0_ln2_b,
                                 tm=tm_ln, max_tk=LN_K_TILE, packed=False, name="l0_ffn_down_ln")
    uv, l1_w_out_b = matmul(x2b, l1_mix_w_in, l1_mix_b_in, tm=tm_mm, tn=MM_COL_TILE, name="l1_in_proj_gelu",
                            to_round=l1_mix_w_out)
    gated = spatial_gate(uv, l1_sg_norm_g, l1_sg_norm_b, l1_sg_w, l1_sg_b)
    x3, x3p = matmul_residual_ln(gated, l1_w_out_b, x2, l1_ln1_g, l1_ln1_b,
                                 tm=tm_ln, max_tk=LN_K_TILE, packed=True, name="l1_out_proj_ln")
    y = moe_layer(x3, x3p, l1_router_w, l1_router_b, l1_exp_w1, l1_exp_w3, l1_exp_w2, l1_ln2_g, l1_ln2_b,
                  tm_route=min(ROUTER_ROW_TILE, S), tm=EXPERT_ROW_TILE,
                  tn_up=min(EXPERT_UP_COL_TILE, l1_exp_w1.shape[2]), tn_down=min(EXPERT_DOWN_COL_TILE, D),
                  tm_combine=min(COMBINE_ROW_TILE, S))
    return y.reshape(B, S, D)
```
